```python
import jax, jax.numpy as jnp
from jax import lax
import numpy as np

D_MODEL = 1024
BATCH = 4
SEQ = 8192
DEPTH = 2

D_MIX = D_MODEL
SB_WIDTH = D_MIX // 2
CONV_WIDTH = D_MIX - SB_WIDTH
SB_HEAD_DIM = 64
SB_HEADS = SB_WIDTH // SB_HEAD_DIM
CONV_K = 3
Q_BLOCK = 128
D_FF = 7 * D_MODEL // 2
N_EXPERTS = 8
TOP_K = 2
D_PLE = 256
N_DENSE = (DEPTH + 1) // 2
N_MOE = DEPTH // 2
D_IN_PROJ = 3 * SB_WIDTH + 3 * CONV_WIDTH
EPS = 1e-6

kernel_name = "hybrid_stickbreak_shortconv_moe_ple"


def rmsnorm(x, g):
    xf = x.astype(jnp.float32)
    y = xf * lax.rsqrt(jnp.mean(xf * xf, axis=-1, keepdims=True) + EPS)
    return (y * g.astype(jnp.float32)).astype(x.dtype)


def stick_breaking_attention(q, k, v):
    bsz, nh, s_len, dh = q.shape
    dtype = q.dtype
    qf = q.astype(jnp.float32)
    kf = k.astype(jnp.float32)
    vf = v.astype(jnp.float32)
    scale = 1.0 / np.sqrt(dh).astype(np.float32)
    n_blocks = s_len // Q_BLOCK
    q_blocks = qf.reshape(bsz, nh, n_blocks, Q_BLOCK, dh).transpose(2, 0, 1, 3, 4)
    key_pos = jnp.arange(s_len)

    def one_block(args):
        q_blk, blk_idx = args
        z = jnp.einsum('bhqd,bhkd->bhqk', q_blk, kf) * scale
        q_pos = blk_idx * Q_BLOCK + jnp.arange(Q_BLOCK)
        causal = key_pos[None, :] < q_pos[:, None]
        log_beta = jax.nn.log_sigmoid(z)
        log_one_minus = jnp.where(causal, jax.nn.log_sigmoid(-z), 0.0)
        later = lax.cumsum(log_one_minus, axis=3, reverse=True) - log_one_minus
        weights = jnp.where(causal, jnp.exp(log_beta + later), 0.0)
        return jnp.einsum('bhqk,bhkd->bhqd', weights, vf)

    out = lax.map(one_block, (q_blocks, jnp.arange(n_blocks)))
    out = out.transpose(1, 2, 0, 3, 4).reshape(bsz, nh, s_len, dh)
    return out.astype(dtype)


def causal_depthwise_conv(u, w, bias):
    ch = u.shape[-1]
    rhs = w.reshape(CONV_K, 1, ch).astype(u.dtype)
    out = lax.conv_general_dilated(
        u, rhs, window_strides=(1,), padding=[(CONV_K - 1, 0)],
        dimension_numbers=('NWC', 'WIO', 'NWC'), feature_group_count=ch)
    return out + bias.astype(u.dtype)


def swiglu(x, w1, w3, w2):
    return (jax.nn.silu(x @ w1) * (x @ w3)) @ w2


def moe_swiglu(x, router_w, w1, w3, w2):
    bsz, s_len, d = x.shape
    xt = x.reshape(bsz * s_len, d)
    logits = (xt @ router_w).astype(jnp.float32)
    top_vals, top_idx = lax.top_k(logits, TOP_K)
    gates = jax.nn.softmax(top_vals, axis=-1)
    combine = jnp.sum(jax.nn.one_hot(top_idx, N_EXPERTS, dtype=jnp.float32)
                      * gates[..., None], axis=1).astype(x.dtype)
    out = jnp.zeros_like(xt)
    for e in range(N_EXPERTS):
        out = out + combine[:, e:e + 1] * swiglu(xt, w1[e], w3[e], w2[e])
    return out.reshape(bsz, s_len, d)


def setup_inputs(seed: int = 0) -> dict:
    key = jax.random.key(seed)
    ks = jax.random.split(key, 24)

    def nrm(k, shape, scale):
        return jax.random.normal(k, shape, jnp.float32) * scale

    def gain(k, shape):
        return 1.0 + 0.02 * jax.random.normal(k, shape, jnp.float32)

    return {
        "x": nrm(ks[0], (BATCH, SEQ, D_MODEL), 1.0),
        "p": nrm(ks[1], (DEPTH, BATCH, SEQ, D_PLE), 1.0),
        "mix_norm_g": gain(ks[2], (DEPTH, D_MODEL)),
        "w_in": nrm(ks[3], (DEPTH, D_MODEL, D_IN_PROJ), D_MODEL ** -0.5),
        "q_norm_g": gain(ks[4], (DEPTH, SB_HEAD_DIM)),
        "k_norm_g": gain(ks[5], (DEPTH, SB_HEAD_DIM)),
        "conv_w": nrm(ks[6], (DEPTH, CONV_K, CONV_WIDTH), CONV_K ** -0.5),
        "conv_b": nrm(ks[7], (DEPTH, CONV_WIDTH), 0.02),
        "attn_out_g": gain(ks[8], (DEPTH, SB_WIDTH)),
        "conv_out_g": gain(ks[9], (DEPTH, CONV_WIDTH)),
        "w_o": nrm(ks[10], (DEPTH, D_MIX, D_MODEL), D_MIX ** -0.5),
        "ffn_norm_g": gain(ks[11], (DEPTH, D_MODEL)),
        "dense_w1": nrm(ks[12], (N_DENSE, D_MODEL, D_FF), D_MODEL ** -0.5),
        "dense_w3": nrm(ks[13], (N_DENSE, D_MODEL, D_FF), D_MODEL ** -0.5),
        "dense_w2": nrm(ks[14], (N_DENSE, D_FF, D_MODEL), D_FF ** -0.5),
        "router_w": nrm(ks[15], (N_MOE, D_MODEL, N_EXPERTS), D_MODEL ** -0.5),
        "moe_w1": nrm(ks[16], (N_MOE, N_EXPERTS, D_MODEL, D_FF), D_MODEL ** -0.5),
        "moe_w3": nrm(ks[17], (N_MOE, N_EXPERTS, D_MODEL, D_FF), D_MODEL ** -0.5),
        "moe_w2": nrm(ks[18], (N_MOE, N_EXPERTS, D_FF, D_MODEL), D_FF ** -0.5),
        "ple_norm_g": gain(ks[19], (DEPTH, D_MODEL)),
        "ple_gate_w": nrm(ks[20], (DEPTH, D_MODEL, D_MODEL), D_MODEL ** -0.5),
        "ple_proj_w": nrm(ks[21], (DEPTH, D_PLE, D_MODEL), D_PLE ** -0.5),
    }


def reference(x, p, mix_norm_g, w_in, q_norm_g, k_norm_g, conv_w, conv_b,
              attn_out_g, conv_out_g, w_o, ffn_norm_g, dense_w1, dense_w3, dense_w2,
              router_w, moe_w1, moe_w3, moe_w2, ple_norm_g, ple_gate_w, ple_proj_w):
    bsz, s_len, _ = x.shape
    h = x
    for i in range(DEPTH):
        a = rmsnorm(h, mix_norm_g[i])
        proj = a @ w_in[i]
        q, k, v = (proj[..., 0:SB_WIDTH],
                   proj[..., SB_WIDTH:2 * SB_WIDTH],
                   proj[..., 2 * SB_WIDTH:3 * SB_WIDTH])
        off = 3 * SB_WIDTH
        u, c_gate, b_gate = (proj[..., off:off + CONV_WIDTH],
                             proj[..., off + CONV_WIDTH:off + 2 * CONV_WIDTH],
                             proj[..., off + 2 * CONV_WIDTH:off + 3 * CONV_WIDTH])

        def heads(t):
            return t.reshape(bsz, s_len, SB_HEADS, SB_HEAD_DIM).transpose(0, 2, 1, 3)
        qh = rmsnorm(heads(q), q_norm_g[i])
        kh = rmsnorm(heads(k), k_norm_g[i])
        vh = heads(v)
        attn = stick_breaking_attention(qh, kh, vh)
        attn = attn.transpose(0, 2, 1, 3).reshape(bsz, s_len, SB_WIDTH)

        conv = b_gate * causal_depthwise_conv(c_gate * u, conv_w[i], conv_b[i])

        mixed = jnp.concatenate([rmsnorm(attn, attn_out_g[i]),
                                 rmsnorm(conv, conv_out_g[i])], axis=-1)
        h = h + mixed @ w_o[i]

        f = rmsnorm(h, ffn_norm_g[i])
        if i % 2 == 0:
            j = i // 2
            h = h + swiglu(f, dense_w1[j], dense_w3[j], dense_w2[j])
        else:
            j = i // 2
            h = h + moe_swiglu(f, router_w[j], moe_w1[j], moe_w3[j], moe_w2[j])

        gate = jax.nn.sigmoid(rmsnorm(h, ple_norm_g[i]) @ ple_gate_w[i])
        h = h + gate * (p[i] @ ple_proj_w[i])
    return h
```

```python
import functools

import numpy as np
import jax
import jax.numpy as jnp
from jax import lax
from jax.experimental import pallas as pl
from jax.experimental.pallas import tpu as pltpu

F32 = jnp.float32
BF16 = jnp.bfloat16

EPS = 1e-6
HEAD_DIM = 64
CONV_K = 3
TOP_K = 2
LANES = 128
HALO_ROWS = 8
LOG2E = float(np.log2(np.e))

VMEM_LIMIT_BYTES = 56 * 1024 * 1024

ROW_TILE = 512
ATTN_TILE = 256
FFN_ROW_TILE = 1024
FFN_COL_TILE = 512


def _rms(x, g):
    return x * lax.rsqrt(jnp.mean(x * x, axis=-1, keepdims=True) + EPS) * g


def _dot(a, b):
    return jnp.dot(a, b, preferred_element_type=F32)


def _mix_in_kernel(h_ref, g_ref, w_ref, qg_ref, kg_ref, hm_ref,
                   q_ref, k_ref, v_ref, cu_ref, b_ref, *, sb):
    a = _rms(h_ref[...], g_ref[...]).astype(BF16)
    hm = hm_ref[...]

    def qk_norm(t, g):
        ms = _dot((t * t).astype(BF16), hm)
        return (t * lax.rsqrt(ms + EPS) * g).astype(BF16)

    q_ref[...] = qk_norm(_dot(a, w_ref[:, 0:sb]), qg_ref[...])
    k_ref[...] = qk_norm(_dot(a, w_ref[:, sb:2 * sb]), kg_ref[...])
    v_ref[...] = _dot(a, w_ref[:, 2 * sb:3 * sb]).astype(BF16)
    u = _dot(a, w_ref[:, 3 * sb:4 * sb])
    c = _dot(a, w_ref[:, 4 * sb:5 * sb])
    cu_ref[...] = c * u
    b_ref[...] = _dot(a, w_ref[:, 5 * sb:6 * sb])


def _mix_in(h, g, w_in, qg, kg, head_mean, *, tm):
    n, d = h.shape
    sb = head_mean.shape[0]
    row = lambda i: (i, 0)
    fixed = lambda i: (0, 0)
    out_bf = jax.ShapeDtypeStruct((n, sb), BF16)
    out_f32 = jax.ShapeDtypeStruct((n, sb), F32)
    return pl.pallas_call(
        functools.partial(_mix_in_kernel, sb=sb),
        grid=(n // tm,),
        in_specs=[
            pl.BlockSpec((tm, d), row),
            pl.BlockSpec((1, d), fixed),
            pl.BlockSpec(w_in.shape, fixed),
            pl.BlockSpec((1, sb), fixed),
            pl.BlockSpec((1, sb), fixed),
            pl.BlockSpec((sb, sb), fixed),
        ],
        out_specs=[pl.BlockSpec((tm, sb), row)] * 5,
        out_shape=[out_bf, out_bf, out_bf, out_f32, out_f32],
        compiler_params=pltpu.CompilerParams(
            dimension_semantics=("arbitrary",), vmem_limit_bytes=VMEM_LIMIT_BYTES),
        name="mix_in",
    )(h, g, w_in, qg, kg, head_mean)


def _attn_kernel(q_ref, k_ref, v_ref, u_ref, o_ref, acc_ref, c_ref, *, t):
    i = pl.program_id(2)
    q = q_ref[0]
    lane = lax.broadcasted_iota(jnp.int32, q.shape, 1)
    q_heads = (jnp.where(lane < HEAD_DIM, q, jnp.zeros_like(q)),
               jnp.where(lane >= HEAD_DIM, q, jnp.zeros_like(q)))
    upper = u_ref[...]
    row = lax.broadcasted_iota(jnp.int32, (t, t), 0)
    col = lax.broadcasted_iota(jnp.int32, (t, t), 1)
    causal = col < row

    acc_ref[...] = jnp.zeros_like(acc_ref)
    c_ref[...] = jnp.zeros_like(c_ref)

    def block(j, diagonal):
        start = pl.multiple_of(j * t, t)
        kj = k_ref[0, pl.ds(start, t), :]
        vj = v_ref[0, pl.ds(start, t), :]
        for hd in range(2):
            z = lax.dot_general(q_heads[hd], kj, (((1,), (1,)), ((), ())),
                                preferred_element_type=F32)
            sp = jnp.maximum(z, 0.0) + jnp.log2(1.0 + jnp.exp2(-jnp.abs(z)))
            log_beta = z - sp
            if diagonal:
                sp = jnp.where(causal, sp, 0.0)
            later = _dot(sp.astype(BF16), upper)
            w = jnp.exp2(log_beta - later - c_ref[hd])
            if diagonal:
                w = jnp.where(causal, w, 0.0)
            acc_ref[hd] += _dot(w.astype(BF16), vj)
            c_ref[hd] += jnp.sum(sp, axis=-1, keepdims=True)

    block(i, True)

    def body(n, carry):
        block(i - 1 - n, False)
        return carry

    lax.fori_loop(0, i, body, 0)
    o_ref[0] = jnp.where(lane < HEAD_DIM, acc_ref[0], acc_ref[1]).astype(o_ref.dtype)


def _attention(q, k, v, upper, *, bsz, t):
    n, sb = q.shape
    s = n // bsz
    q3, k3, v3 = (a.reshape(bsz, s, sb) for a in (q, k, v))
    blk = lambda b, hp, i: (b, i, hp)
    seq = lambda b, hp, i: (b, 0, hp)
    out = pl.pallas_call(
        functools.partial(_attn_kernel, t=t),
        grid=(bsz, sb // LANES, s // t),
        in_specs=[
            pl.BlockSpec((1, t, LANES), blk),
            pl.BlockSpec((1, s, LANES), seq),
            pl.BlockSpec((1, s, LANES), seq),
            pl.BlockSpec((t, t), lambda b, hp, i: (0, 0)),
        ],
        out_specs=pl.BlockSpec((1, t, LANES), blk),
        out_shape=jax.ShapeDtypeStruct((bsz, s, sb), BF16),
        scratch_shapes=[pltpu.VMEM((2, t, LANES), F32), pltpu.VMEM((2, t, 1), F32)],
        compiler_params=pltpu.CompilerParams(
            dimension_semantics=("arbitrary", "arbitrary", "arbitrary"),
            vmem_limit_bytes=VMEM_LIMIT_BYTES),
        name="attention",
    )(q3, k3, v3, upper)
    return out.reshape(n, sb)


def _mix_out_kernel(*refs, tiles_per_seq, sb, n_experts):
    (attn_ref, cu_ref, halo_ref, b_ref, h_ref, ag_ref, cg_ref, cw_ref, cb_ref,
     wo_ref, fg_ref) = refs[:11]
    if n_experts:
        rw_ref, h1_ref, f_ref, cmb_ref = refs[11:]
    else:
        h1_ref, f_ref = refs[11:]

    attn_n = _rms(attn_ref[...].astype(F32), ag_ref[...])

    cu = cu_ref[...]
    first = pl.program_id(0) % tiles_per_seq == 0
    halo = jnp.where(first, 0.0, halo_ref[...])
    rowi = lax.broadcasted_iota(jnp.int32, cu.shape, 0)
    prev1 = jnp.where(rowi == 0, halo[HALO_ROWS - 1:HALO_ROWS], pltpu.roll(cu, 1, axis=0))
    prev2 = jnp.where(rowi == 0, halo[HALO_ROWS - 2:HALO_ROWS - 1],
                      jnp.where(rowi == 1, halo[HALO_ROWS - 1:HALO_ROWS],
                                pltpu.roll(cu, 2, axis=0)))
    conv = cw_ref[0:1] * prev2 + cw_ref[1:2] * prev1 + cw_ref[2:3] * cu + cb_ref[...]
    conv_n = _rms(b_ref[...] * conv, cg_ref[...])

    h1 = (h_ref[...] + _dot(attn_n.astype(BF16), wo_ref[0:sb])
          + _dot(conv_n.astype(BF16), wo_ref[sb:]))
    h1_ref[...] = h1
    f = _rms(h1, fg_ref[...])
    f_ref[...] = f.astype(BF16)

    if n_experts:
        logits = jnp.dot(f, rw_ref[...], preferred_element_type=F32,
                         precision=lax.Precision.HIGHEST)
        lane = lax.broadcasted_iota(jnp.int32, logits.shape, 1)
        neg = jnp.float32(-jnp.inf)
        logits = jnp.where(lane < n_experts, logits, neg)
        m1 = jnp.max(logits, axis=-1, keepdims=True)
        i1 = jnp.min(jnp.where(logits == m1, lane, LANES), axis=-1, keepdims=True)
        rest = jnp.where(lane == i1, neg, logits)
        m2 = jnp.max(rest, axis=-1, keepdims=True)
        i2 = jnp.min(jnp.where(rest == m2, lane, LANES), axis=-1, keepdims=True)
        e2 = jnp.exp(m2 - m1)
        g1 = 1.0 / (1.0 + e2)
        g2 = e2 * g1
        cmb_ref[...] = jnp.where(lane == i1, g1, 0.0) + jnp.where(lane == i2, g2, 0.0)


def _mix_out(attn, cu, b, h, ag, cg, cw, cb, w_o, fg, router_w, *, tm, seq_len, n_experts):
    n, d = h.shape
    sb = attn.shape[1]
    row = lambda i: (i, 0)
    fixed = lambda i: (0, 0)
    halo = lambda i: (jnp.maximum(i * (tm // HALO_ROWS) - 1, 0), 0)
    in_specs = [
        pl.BlockSpec((tm, sb), row),
        pl.BlockSpec((tm, sb), row),
        pl.BlockSpec((HALO_ROWS, sb), halo),
        pl.BlockSpec((tm, sb), row),
        pl.BlockSpec((tm, d), row),
        pl.BlockSpec((1, sb), fixed),
        pl.BlockSpec((1, sb), fixed),
        pl.BlockSpec((CONV_K, sb), fixed),
        pl.BlockSpec((1, sb), fixed),
        pl.BlockSpec(w_o.shape, fixed),
        pl.BlockSpec((1, d), fixed),
    ]
    args = [attn, cu, cu, b, h, ag, cg, cw, cb, w_o, fg]
    out_specs = [pl.BlockSpec((tm, d), row), pl.BlockSpec((tm, d), row)]
    out_shape = [jax.ShapeDtypeStruct((n, d), F32), jax.ShapeDtypeStruct((n, d), BF16)]
    if n_experts:
        in_specs.append(pl.BlockSpec(router_w.shape, fixed))
        args.append(router_w)
        out_specs.append(pl.BlockSpec((tm, LANES), row))
        out_shape.append(jax.ShapeDtypeStruct((n, LANES), F32))
    return pl.pallas_call(
        functools.partial(_mix_out_kernel, tiles_per_seq=seq_len // tm, sb=sb,
                          n_experts=n_experts),
        grid=(n // tm,),
        in_specs=in_specs,
        out_specs=out_specs,
        out_shape=out_shape,
        compiler_params=pltpu.CompilerParams(
            dimension_semantics=("arbitrary",), vmem_limit_bytes=VMEM_LIMIT_BYTES),
        name="mix_out_moe" if n_experts else "mix_out",
    )(*args)


def _ffn_kernel(*refs, routed):
    if routed:
        (f_ref, cmb_ref, w1_ref, w3_ref, w2_ref, h1_ref, p_ref, pg_ref, wg_ref, wp_ref,
         o_ref, acc_ref) = refs
    else:
        (f_ref, w1_ref, w3_ref, w2_ref, h1_ref, p_ref, pg_ref, wg_ref, wp_ref,
         o_ref, acc_ref) = refs
    e = pl.program_id(1)
    c = pl.program_id(2)

    @pl.when((e == 0) & (c == 0))
    def _():
        acc_ref[...] = jnp.zeros_like(acc_ref)

    x = f_ref[...]
    a = _dot(x, w1_ref[0])
    hidden = a * jax.nn.sigmoid(a) * _dot(x, w3_ref[0])
    if routed:
        cmb = cmb_ref[...]
        lane = lax.broadcasted_iota(jnp.int32, cmb.shape, 1)
        gate_e = jnp.sum(jnp.where(lane == e, cmb, 0.0), axis=-1, keepdims=True)
        hidden = hidden * gate_e
    acc_ref[...] += _dot(hidden.astype(BF16), w2_ref[0])

    @pl.when((e == pl.num_programs(1) - 1) & (c == pl.num_programs(2) - 1))
    def _():
        h2 = h1_ref[...] + acc_ref[...]
        gate = jax.nn.sigmoid(_dot(_rms(h2, pg_ref[...]).astype(BF16), wg_ref[...]))
        o_ref[...] = h2 + gate * _dot(p_ref[...].astype(BF16), wp_ref[...])


def _ffn(f, cmb, w1, w3, w2, h1, p, pg, wg, wp, *, tm, tf):
    n, d = h1.shape
    n_exp, _, d_ff = w1.shape
    routed = cmb is not None
    row = lambda i, e, c: (i, 0)
    fixed = lambda i, e, c: (0, 0)
    in_specs = [pl.BlockSpec((tm, d), row)]
    args = [f]
    if routed:
        in_specs.append(pl.BlockSpec((tm, LANES), row))
        args.append(cmb)
    in_specs += [
        pl.BlockSpec((1, d, tf), lambda i, e, c: (e, 0, c)),
        pl.BlockSpec((1, d, tf), lambda i, e, c: (e, 0, c)),
        pl.BlockSpec((1, tf, d), lambda i, e, c: (e, c, 0)),
        pl.BlockSpec((tm, d), row),
        pl.BlockSpec((tm, p.shape[1]), row),
        pl.BlockSpec((1, d), fixed),
        pl.BlockSpec(wg.shape, fixed),
        pl.BlockSpec(wp.shape, fixed),
    ]
    args += [w1, w3, w2, h1, p, pg, wg, wp]
    return pl.pallas_call(
        functools.partial(_ffn_kernel, routed=routed),
        grid=(n // tm, n_exp, d_ff // tf),
        in_specs=in_specs,
        out_specs=pl.BlockSpec((tm, d), row),
        out_shape=jax.ShapeDtypeStruct((n, d), F32),
        scratch_shapes=[pltpu.VMEM((tm, d), F32)],
        compiler_params=pltpu.CompilerParams(
            dimension_semantics=("arbitrary", "arbitrary", "arbitrary"),
            vmem_limit_bytes=VMEM_LIMIT_BYTES),
        name="ffn_moe" if routed else "ffn_dense",
    )(*args)


def kernel(x, p, mix_norm_g, w_in, q_norm_g, k_norm_g, conv_w, conv_b, attn_out_g, conv_out_g,
           w_o, ffn_norm_g, dense_w1, dense_w3, dense_w2, router_w, moe_w1, moe_w3, moe_w2,
           ple_norm_g, ple_gate_w, ple_proj_w):
    bsz, seq_len, d = x.shape
    depth = w_in.shape[0]
    sb = attn_out_g.shape[1]
    n_heads = sb // HEAD_DIM
    n_experts = router_w.shape[2]
    n = bsz * seq_len
    tm = min(ROW_TILE, seq_len)
    t = min(ATTN_TILE, seq_len)
    tm_ffn = min(FFN_ROW_TILE, n)

    head_id = jnp.arange(sb) // HEAD_DIM
    head_mean = jnp.where(head_id[:, None] == head_id[None, :], 1.0 / HEAD_DIM, 0.0).astype(BF16)
    ar = jnp.arange(t)
    upper = (ar[:, None] > ar[None, :]).astype(BF16)
    q_scale = LOG2E / float(np.sqrt(HEAD_DIM))
    row2 = lambda a: a.reshape(1, -1)

    h = x.reshape(n, d)
    for i in range(depth):
        qg = row2(jnp.tile(q_norm_g[i], n_heads) * q_scale)
        kg = row2(jnp.tile(k_norm_g[i], n_heads))
        q, k, v, cu, b = _mix_in(h, row2(mix_norm_g[i]), w_in[i].astype(BF16), qg, kg,
                                 head_mean, tm=tm)
        attn = _attention(q, k, v, upper, bsz=bsz, t=t)
        j = i // 2
        routed = i % 2 == 1
        rw = None
        if routed:
            rw = jnp.zeros((d, LANES), F32).at[:, :n_experts].set(router_w[j])
        outs = _mix_out(attn, cu, b, h, row2(attn_out_g[i]), row2(conv_out_g[i]), conv_w[i],
                        row2(conv_b[i]), w_o[i].astype(BF16), row2(ffn_norm_g[i]), rw,
                        tm=tm, seq_len=seq_len, n_experts=n_experts if routed else 0)
        if routed:
            h1, f, cmb = outs
            w1, w3, w2 = moe_w1[j], moe_w3[j], moe_w2[j]
        else:
            h1, f = outs
            cmb = None
            w1, w3, w2 = dense_w1[j:j + 1], dense_w3[j:j + 1], dense_w2[j:j + 1]
        h = _ffn(f, cmb, w1.astype(BF16), w3.astype(BF16), w2.astype(BF16), h1,
                 p[i].reshape(n, -1), row2(ple_norm_g[i]), ple_gate_w[i].astype(BF16),
                 ple_proj_w[i].astype(BF16), tm=tm_ffn, tf=FFN_COL_TILE)
    return h.reshape(bsz, seq_len, d)
```

```python
import functools

import numpy as np
import jax
import jax.numpy as jnp
from jax import lax
from jax.experimental import pallas as pl
from jax.experimental.pallas import tpu as pltpu

F32 = jnp.float32
BF16 = jnp.bfloat16

EPS = 1e-6
HEAD_DIM = 64
CONV_K = 3
TOP_K = 2
LANES = 128
HALO_ROWS = 8
LOG2E = float(np.log2(np.e))

VMEM_LIMIT_BYTES = 56 * 1024 * 1024

ROW_TILE = 512
ATTN_TILE = 256
ATTN_PAIRS = 2
SP_CLAMP = 64.0
DONE_LOG2 = 160.0
FFN_ROW_TILE = 1024
FFN_COL_TILE = 512


def _rms(x, g):
    return x * lax.rsqrt(jnp.mean(x * x, axis=-1, keepdims=True) + EPS) * g


def _dot(a, b):
    return jnp.dot(a, b, preferred_element_type=F32)


def _mix_in_kernel(h_ref, g_ref, w_ref, qg_ref, kg_ref, hm_ref,
                   q_ref, k_ref, v_ref, cu_ref, b_ref, *, sb):
    a = _rms(h_ref[...], g_ref[...]).astype(BF16)
    hm = hm_ref[...]

    def qk_norm(t, g):
        ms = _dot((t * t).astype(BF16), hm)
        return (t * lax.rsqrt(ms + EPS) * g).astype(BF16)

    q_ref[...] = qk_norm(_dot(a, w_ref[:, 0:sb]), qg_ref[...])
    k_ref[...] = qk_norm(_dot(a, w_ref[:, sb:2 * sb]), kg_ref[...])
    v_ref[...] = _dot(a, w_ref[:, 2 * sb:3 * sb]).astype(BF16)
    u = _dot(a, w_ref[:, 3 * sb:4 * sb])
    c = _dot(a, w_ref[:, 4 * sb:5 * sb])
    cu_ref[...] = c * u
    b_ref[...] = _dot(a, w_ref[:, 5 * sb:6 * sb])


def _mix_in(h, g, w_in, qg, kg, head_mean, *, tm):
    n, d = h.shape
    sb = head_mean.shape[0]
    row = lambda i: (i, 0)
    fixed = lambda i: (0, 0)
    out_bf = jax.ShapeDtypeStruct((n, sb), BF16)
    out_f32 = jax.ShapeDtypeStruct((n, sb), F32)
    return pl.pallas_call(
        functools.partial(_mix_in_kernel, sb=sb),
        grid=(n // tm,),
        in_specs=[
            pl.BlockSpec((tm, d), row),
            pl.BlockSpec((1, d), fixed),
            pl.BlockSpec(w_in.shape, fixed),
            pl.BlockSpec((1, sb), fixed),
            pl.BlockSpec((1, sb), fixed),
            pl.BlockSpec((sb, sb), fixed),
        ],
        out_specs=[pl.BlockSpec((tm, sb), row)] * 5,
        out_shape=[out_bf, out_bf, out_bf, out_f32, out_f32],
        compiler_params=pltpu.CompilerParams(
            dimension_semantics=("arbitrary",), vmem_limit_bytes=VMEM_LIMIT_BYTES),
        name="mix_in",
    )(h, g, w_in, qg, kg, head_mean)


def _attn_kernel(q_ref, k_ref, v_ref, u_ref, o_ref, acc_ref, c_ref, *, t, pairs):
    i = pl.program_id(2)
    heads = 2 * pairs
    q = q_ref[0]
    lane = lax.broadcasted_iota(jnp.int32, (t, LANES), 1)
    lo_half = lane < HEAD_DIM
    q_heads = []
    for p in range(pairs):
        qp = q[:, p * LANES:(p + 1) * LANES]
        q_heads += [jnp.where(lo_half, qp, jnp.zeros_like(qp)),
                    jnp.where(lo_half, jnp.zeros_like(qp), qp)]
    tri = u_ref[...]
    row = lax.broadcasted_iota(jnp.int32, (t, t), 0)
    col = lax.broadcasted_iota(jnp.int32, (t, t), 1)
    causal = col < row

    acc_ref[...] = jnp.zeros_like(acc_ref)
    c_ref[...] = jnp.zeros_like(c_ref)

    def block(j, diagonal):
        start = pl.multiple_of(j * t, t)
        kj = k_ref[0, pl.ds(start, t), :]
        vj = v_ref[0, pl.ds(start, t), :]
        for hd in range(heads):
            p = hd // 2
            z = lax.dot_general(q_heads[hd], kj[:, p * LANES:(p + 1) * LANES],
                                (((1,), (1,)), ((), ())), preferred_element_type=F32)
            sp = jnp.maximum(z, jnp.log2(1.0 + jnp.exp2(jnp.minimum(z, SP_CLAMP))))
            if diagonal:
                sp = jnp.where(causal, sp, 0.0)
            incl = _dot(sp.astype(BF16), tri)
            w = jnp.exp2(z - incl - c_ref[hd])
            if diagonal:
                w = jnp.where(causal, w, 0.0)
            acc_ref[hd] += _dot(w.astype(BF16), vj[:, p * LANES:(p + 1) * LANES])
            c_ref[hd] += incl[:, 0:1]

    def decay_floor():
        m = jnp.min(c_ref[0])
        for hd in range(1, heads):
            m = jnp.minimum(m, jnp.min(c_ref[hd]))
        return m

    block(i, True)

    def cond(carry):
        j, floor = carry
        return (j >= 0) & (floor < DONE_LOG2)

    def body(carry):
        j, _ = carry
        block(j, False)
        return j - 1, decay_floor()

    lax.while_loop(cond, body, (i - 1, decay_floor()))
    for p in range(pairs):
        o_ref[0, :, p * LANES:(p + 1) * LANES] = jnp.where(
            lo_half, acc_ref[2 * p], acc_ref[2 * p + 1]).astype(o_ref.dtype)


def _attention(q, k, v, tri, *, bsz, t, pairs):
    n, sb = q.shape
    s = n // bsz
    width = LANES * pairs
    q3, k3, v3 = (a.reshape(bsz, s, sb) for a in (q, k, v))
    blk = lambda b, g, i: (b, i, g)
    seq = lambda b, g, i: (b, 0, g)
    out = pl.pallas_call(
        functools.partial(_attn_kernel, t=t, pairs=pairs),
        grid=(bsz, sb // width, s // t),
        in_specs=[
            pl.BlockSpec((1, t, width), blk),
            pl.BlockSpec((1, s, width), seq),
            pl.BlockSpec((1, s, width), seq),
            pl.BlockSpec((t, t), lambda b, g, i: (0, 0)),
        ],
        out_specs=pl.BlockSpec((1, t, width), blk),
        out_shape=jax.ShapeDtypeStruct((bsz, s, sb), BF16),
        scratch_shapes=[pltpu.VMEM((2 * pairs, t, LANES), F32),
                        pltpu.VMEM((2 * pairs, t, 1), F32)],
        compiler_params=pltpu.CompilerParams(
            dimension_semantics=("arbitrary", "arbitrary", "arbitrary"),
            vmem_limit_bytes=VMEM_LIMIT_BYTES),
        name="attention",
    )(q3, k3, v3, tri)
    return out.reshape(n, sb)


def _mix_out_kernel(*refs, tiles_per_seq, sb, n_experts):
    (attn_ref, cu_ref, halo_ref, b_ref, h_ref, ag_ref, cg_ref, cw_ref, cb_ref,
     wo_ref, fg_ref) = refs[:11]
    if n_experts:
        rw_ref, h1_ref, f_ref, cmb_ref = refs[11:]
    else:
        h1_ref, f_ref = refs[11:]

    attn_n = _rms(attn_ref[...].astype(F32), ag_ref[...])

    cu = cu_ref[...]
    first = pl.program_id(0) % tiles_per_seq == 0
    halo = jnp.where(first, 0.0, halo_ref[...])
    rowi = lax.broadcasted_iota(jnp.int32, cu.shape, 0)
    prev1 = jnp.where(rowi == 0, halo[HALO_ROWS - 1:HALO_ROWS], pltpu.roll(cu, 1, axis=0))
    prev2 = jnp.where(rowi == 0, halo[HALO_ROWS - 2:HALO_ROWS - 1],
                      jnp.where(rowi == 1, halo[HALO_ROWS - 1:HALO_ROWS],
                                pltpu.roll(cu, 2, axis=0)))
    conv = cw_ref[0:1] * prev2 + cw_ref[1:2] * prev1 + cw_ref[2:3] * cu + cb_ref[...]
    conv_n = _rms(b_ref[...] * conv, cg_ref[...])

    h1 = (h_ref[...] + _dot(attn_n.astype(BF16), wo_ref[0:sb])
          + _dot(conv_n.astype(BF16), wo_ref[sb:]))
    h1_ref[...] = h1
    f = _rms(h1, fg_ref[...])
    f_ref[...] = f.astype(BF16)

    if n_experts:
        logits = jnp.dot(f, rw_ref[...], preferred_element_type=F32,
                         precision=lax.Precision.HIGHEST)
        lane = lax.broadcasted_iota(jnp.int32, logits.shape, 1)
        neg = jnp.float32(-jnp.inf)
        logits = jnp.where(lane < n_experts, logits, neg)
        m1 = jnp.max(logits, axis=-1, keepdims=True)
        i1 = jnp.min(jnp.where(logits == m1, lane, LANES), axis=-1, keepdims=True)
        rest = jnp.where(lane == i1, neg, logits)
        m2 = jnp.max(rest, axis=-1, keepdims=True)
        i2 = jnp.min(jnp.where(rest == m2, lane, LANES), axis=-1, keepdims=True)
        e2 = jnp.exp(m2 - m1)
        g1 = 1.0 / (1.0 + e2)
        g2 = e2 * g1
        cmb_ref[...] = jnp.where(lane == i1, g1, 0.0) + jnp.where(lane == i2, g2, 0.0)


def _mix_out(attn, cu, b, h, ag, cg, cw, cb, w_o, fg, router_w, *, tm, seq_len, n_experts):
    n, d = h.shape
    sb = attn.shape[1]
    row = lambda i: (i, 0)
    fixed = lambda i: (0, 0)
    halo = lambda i: (jnp.maximum(i * (tm // HALO_ROWS) - 1, 0), 0)
    in_specs = [
        pl.BlockSpec((tm, sb), row),
        pl.BlockSpec((tm, sb), row),
        pl.BlockSpec((HALO_ROWS, sb), halo),
        pl.BlockSpec((tm, sb), row),
        pl.BlockSpec((tm, d), row),
        pl.BlockSpec((1, sb), fixed),
        pl.BlockSpec((1, sb), fixed),
        pl.BlockSpec((CONV_K, sb), fixed),
        pl.BlockSpec((1, sb), fixed),
        pl.BlockSpec(w_o.shape, fixed),
        pl.BlockSpec((1, d), fixed),
    ]
    args = [attn, cu, cu, b, h, ag, cg, cw, cb, w_o, fg]
    out_specs = [pl.BlockSpec((tm, d), row), pl.BlockSpec((tm, d), row)]
    out_shape = [jax.ShapeDtypeStruct((n, d), F32), jax.ShapeDtypeStruct((n, d), BF16)]
    if n_experts:
        in_specs.append(pl.BlockSpec(router_w.shape, fixed))
        args.append(router_w)
        out_specs.append(pl.BlockSpec((tm, LANES), row))
        out_shape.append(jax.ShapeDtypeStruct((n, LANES), F32))
    return pl.pallas_call(
        functools.partial(_mix_out_kernel, tiles_per_seq=seq_len // tm, sb=sb,
                          n_experts=n_experts),
        grid=(n // tm,),
        in_specs=in_specs,
        out_specs=out_specs,
        out_shape=out_shape,
        compiler_params=pltpu.CompilerParams(
            dimension_semantics=("arbitrary",), vmem_limit_bytes=VMEM_LIMIT_BYTES),
        name="mix_out_moe" if n_experts else "mix_out",
    )(*args)


def _ffn_kernel(*refs, routed):
    if routed:
        (f_ref, cmb_ref, w1_ref, w3_ref, w2_ref, h1_ref, p_ref, pg_ref, wg_ref, wp_ref,
         o_ref, acc_ref) = refs
    else:
        (f_ref, w1_ref, w3_ref, w2_ref, h1_ref, p_ref, pg_ref, wg_ref, wp_ref,
         o_ref, acc_ref) = refs
    e = pl.program_id(1)
    c = pl.program_id(2)

    @pl.when((e == 0) & (c == 0))
    def _():
        acc_ref[...] = jnp.zeros_like(acc_ref)

    x = f_ref[...]
    a = _dot(x, w1_ref[0])
    hidden = a * jax.nn.sigmoid(a) * _dot(x, w3_ref[0])
    if routed:
        cmb = cmb_ref[...]
        lane = lax.broadcasted_iota(jnp.int32, cmb.shape, 1)
        gate_e = jnp.sum(jnp.where(lane == e, cmb, 0.0), axis=-1, keepdims=True)
        hidden = hidden * gate_e
    acc_ref[...] += _dot(hidden.astype(BF16), w2_ref[0])

    @pl.when((e == pl.num_programs(1) - 1) & (c == pl.num_programs(2) - 1))
    def _():
        h2 = h1_ref[...] + acc_ref[...]
        gate = jax.nn.sigmoid(_dot(_rms(h2, pg_ref[...]).astype(BF16), wg_ref[...]))
        o_ref[...] = h2 + gate * _dot(p_ref[...].astype(BF16), wp_ref[...])


def _ffn(f, cmb, w1, w3, w2, h1, p, pg, wg, wp, *, tm, tf):
    n, d = h1.shape
    n_exp, _, d_ff = w1.shape
    routed = cmb is not None
    row = lambda i, e, c: (i, 0)
    fixed = lambda i, e, c: (0, 0)
    in_specs = [pl.BlockSpec((tm, d), row)]
    args = [f]
    if routed:
        in_specs.append(pl.BlockSpec((tm, LANES), row))
        args.append(cmb)
    in_specs += [
        pl.BlockSpec((1, d, tf), lambda i, e, c: (e, 0, c)),
        pl.BlockSpec((1, d, tf), lambda i, e, c: (e, 0, c)),
        pl.BlockSpec((1, tf, d), lambda i, e, c: (e, c, 0)),
        pl.BlockSpec((tm, d), row),
        pl.BlockSpec((tm, p.shape[1]), row),
        pl.BlockSpec((1, d), fixed),
        pl.BlockSpec(wg.shape, fixed),
        pl.BlockSpec(wp.shape, fixed),
    ]
    args += [w1, w3, w2, h1, p, pg, wg, wp]
    return pl.pallas_call(
        functools.partial(_ffn_kernel, routed=routed),
        grid=(n // tm, n_exp, d_ff // tf),
        in_specs=in_specs,
        out_specs=pl.BlockSpec((tm, d), row),
        out_shape=jax.ShapeDtypeStruct((n, d), F32),
        scratch_shapes=[pltpu.VMEM((tm, d), F32)],
        compiler_params=pltpu.CompilerParams(
            dimension_semantics=("arbitrary", "arbitrary", "arbitrary"),
            vmem_limit_bytes=VMEM_LIMIT_BYTES),
        name="ffn_moe" if routed else "ffn_dense",
    )(*args)


def kernel(x, p, mix_norm_g, w_in, q_norm_g, k_norm_g, conv_w, conv_b, attn_out_g, conv_out_g,
           w_o, ffn_norm_g, dense_w1, dense_w3, dense_w2, router_w, moe_w1, moe_w3, moe_w2,
           ple_norm_g, ple_gate_w, ple_proj_w):
    bsz, seq_len, d = x.shape
    depth = w_in.shape[0]
    sb = attn_out_g.shape[1]
    n_heads = sb // HEAD_DIM
    n_experts = router_w.shape[2]
    n = bsz * seq_len
    tm = min(ROW_TILE, seq_len)
    t = min(ATTN_TILE, seq_len)
    tm_ffn = min(FFN_ROW_TILE, n)

    head_id = jnp.arange(sb) // HEAD_DIM
    head_mean = jnp.where(head_id[:, None] == head_id[None, :], 1.0 / HEAD_DIM, 0.0).astype(BF16)
    ar = jnp.arange(t)
    tri = (ar[:, None] >= ar[None, :]).astype(BF16)
    q_scale = LOG2E / float(np.sqrt(HEAD_DIM))
    row2 = lambda a: a.reshape(1, -1)

    h = x.reshape(n, d)
    for i in range(depth):
        qg = row2(jnp.tile(q_norm_g[i], n_heads) * q_scale)
        kg = row2(jnp.tile(k_norm_g[i], n_heads))
        q, k, v, cu, b = _mix_in(h, row2(mix_norm_g[i]), w_in[i].astype(BF16), qg, kg,
                                 head_mean, tm=tm)
        attn = _attention(q, k, v, tri, bsz=bsz, t=t, pairs=ATTN_PAIRS)
        j = i // 2
        routed = i % 2 == 1
        rw = None
        if routed:
            rw = jnp.zeros((d, LANES), F32).at[:, :n_experts].set(router_w[j])
        outs = _mix_out(attn, cu, b, h, row2(attn_out_g[i]), row2(conv_out_g[i]), conv_w[i],
                        row2(conv_b[i]), w_o[i].astype(BF16), row2(ffn_norm_g[i]), rw,
                        tm=tm, seq_len=seq_len, n_experts=n_experts if routed else 0)
        if routed:
            h1, f, cmb = outs
            w1, w3, w2 = moe_w1[j], moe_w3[j], moe_w2[j]
        else:
            h1, f = outs
            cmb = None
            w1, w3, w2 = dense_w1[j:j + 1], dense_w3[j:j + 1], dense_w2[j:j + 1]
        h = _ffn(f, cmb, w1.astype(BF16), w3.astype(BF16), w2.astype(BF16), h1,
                 p[i].reshape(n, -1), row2(ple_norm_g[i]), ple_gate_w[i].astype(BF16),
                 ple_proj_w[i].astype(BF16), tm=tm_ffn, tf=FFN_COL_TILE)
    return h.reshape(bsz, seq_len, d)
```

```python
import functools

import numpy as np
import jax
import jax.numpy as jnp
from jax import lax
from jax.experimental import pallas as pl
from jax.experimental.pallas import tpu as pltpu

F32 = jnp.float32
BF16 = jnp.bfloat16

EPS = 1e-6
HEAD_DIM = 64
CONV_K = 3
TOP_K = 2
LANES = 128
SUBLANES = 8
HALO_ROWS = SUBLANES
LOG2E = float(np.log2(np.e))

VMEM_LIMIT_BYTES = 56 * 1024 * 1024

ROW_TILE = 512
ATTN_TILE = 256
ATTN_PAIRS = 2
SP_CLAMP = 64.0
DONE_LOG2 = 160.0
FFN_ROW_TILE = 1024
FFN_COL_TILE = 512
MOE_ROW_TILE = 512


def _rms(x, g):
    return x * lax.rsqrt(jnp.mean(x * x, axis=-1, keepdims=True) + EPS) * g


def _dot(a, b):
    return jnp.dot(a, b, preferred_element_type=F32)


def _mix_in_kernel(h_ref, g_ref, w_ref, qg_ref, kg_ref, hm_ref,
                   q_ref, k_ref, v_ref, cu_ref, b_ref, *, sb):
    a = _rms(h_ref[...], g_ref[...]).astype(BF16)
    hm = hm_ref[...]

    def qk_norm(t, g):
        ms = _dot((t * t).astype(BF16), hm)
        return (t * lax.rsqrt(ms + EPS) * g).astype(BF16)

    q_ref[...] = qk_norm(_dot(a, w_ref[:, 0:sb]), qg_ref[...])
    k_ref[...] = qk_norm(_dot(a, w_ref[:, sb:2 * sb]), kg_ref[...])
    v_ref[...] = _dot(a, w_ref[:, 2 * sb:3 * sb]).astype(BF16)
    u = _dot(a, w_ref[:, 3 * sb:4 * sb])
    c = _dot(a, w_ref[:, 4 * sb:5 * sb])
    cu_ref[...] = c * u
    b_ref[...] = _dot(a, w_ref[:, 5 * sb:6 * sb])


def _mix_in(h, g, w_in, qg, kg, head_mean, *, tm):
    n, d = h.shape
    sb = head_mean.shape[0]
    row = lambda i: (i, 0)
    fixed = lambda i: (0, 0)
    out_bf = jax.ShapeDtypeStruct((n, sb), BF16)
    out_f32 = jax.ShapeDtypeStruct((n, sb), F32)
    return pl.pallas_call(
        functools.partial(_mix_in_kernel, sb=sb),
        grid=(n // tm,),
        in_specs=[
            pl.BlockSpec((tm, d), row),
            pl.BlockSpec((1, d), fixed),
            pl.BlockSpec(w_in.shape, fixed),
            pl.BlockSpec((1, sb), fixed),
            pl.BlockSpec((1, sb), fixed),
            pl.BlockSpec((sb, sb), fixed),
        ],
        out_specs=[pl.BlockSpec((tm, sb), row)] * 5,
        out_shape=[out_bf, out_bf, out_bf, out_f32, out_f32],
        compiler_params=pltpu.CompilerParams(
            dimension_semantics=("arbitrary",), vmem_limit_bytes=VMEM_LIMIT_BYTES),
        name="mix_in",
    )(h, g, w_in, qg, kg, head_mean)


def _attn_kernel(q_ref, k_ref, v_ref, u_ref, o_ref, acc_ref, c_ref, *, t, pairs):
    i = pl.program_id(2)
    heads = 2 * pairs
    q = q_ref[0]
    lane = lax.broadcasted_iota(jnp.int32, (t, LANES), 1)
    lo_half = lane < HEAD_DIM
    q_heads = []
    for p in range(pairs):
        qp = q[:, p * LANES:(p + 1) * LANES]
        q_heads += [jnp.where(lo_half, qp, jnp.zeros_like(qp)),
                    jnp.where(lo_half, jnp.zeros_like(qp), qp)]
    tri = u_ref[...]
    row = lax.broadcasted_iota(jnp.int32, (t, t), 0)
    col = lax.broadcasted_iota(jnp.int32, (t, t), 1)
    causal = col < row

    acc_ref[...] = jnp.zeros_like(acc_ref)
    c_ref[...] = jnp.zeros_like(c_ref)

    def block(j, diagonal):
        start = pl.multiple_of(j * t, t)
        kj = k_ref[0, pl.ds(start, t), :]
        vj = v_ref[0, pl.ds(start, t), :]
        for hd in range(heads):
            p = hd // 2
            z = lax.dot_general(q_heads[hd], kj[:, p * LANES:(p + 1) * LANES],
                                (((1,), (1,)), ((), ())), preferred_element_type=F32)
            sp = jnp.maximum(z, jnp.log2(1.0 + jnp.exp2(jnp.minimum(z, SP_CLAMP))))
            if diagonal:
                sp = jnp.where(causal, sp, 0.0)
            incl = _dot(sp.astype(BF16), tri)
            w = jnp.exp2(z - incl - c_ref[hd])
            if diagonal:
                w = jnp.where(causal, w, 0.0)
            acc_ref[hd] += _dot(w.astype(BF16), vj[:, p * LANES:(p + 1) * LANES])
            c_ref[hd] += incl[:, 0:1]

    def decay_floor():
        m = jnp.min(c_ref[0])
        for hd in range(1, heads):
            m = jnp.minimum(m, jnp.min(c_ref[hd]))
        return m

    block(i, True)

    def cond(carry):
        j, floor = carry
        return (j >= 0) & (floor < DONE_LOG2)

    def body(carry):
        j, _ = carry
        block(j, False)
        return j - 1, decay_floor()

    lax.while_loop(cond, body, (i - 1, decay_floor()))
    for p in range(pairs):
        o_ref[0, :, p * LANES:(p + 1) * LANES] = jnp.where(
            lo_half, acc_ref[2 * p], acc_ref[2 * p + 1]).astype(o_ref.dtype)


def _attention(q, k, v, tri, *, bsz, t, pairs):
    n, sb = q.shape
    s = n // bsz
    width = LANES * pairs
    q3, k3, v3 = (a.reshape(bsz, s, sb) for a in (q, k, v))
    blk = lambda b, g, i: (b, i, g)
    seq = lambda b, g, i: (b, 0, g)
    out = pl.pallas_call(
        functools.partial(_attn_kernel, t=t, pairs=pairs),
        grid=(bsz, sb // width, s // t),
        in_specs=[
            pl.BlockSpec((1, t, width), blk),
            pl.BlockSpec((1, s, width), seq),
            pl.BlockSpec((1, s, width), seq),
            pl.BlockSpec((t, t), lambda b, g, i: (0, 0)),
        ],
        out_specs=pl.BlockSpec((1, t, width), blk),
        out_shape=jax.ShapeDtypeStruct((bsz, s, sb), BF16),
        scratch_shapes=[pltpu.VMEM((2 * pairs, t, LANES), F32),
                        pltpu.VMEM((2 * pairs, t, 1), F32)],
        compiler_params=pltpu.CompilerParams(
            dimension_semantics=("arbitrary", "arbitrary", "arbitrary"),
            vmem_limit_bytes=VMEM_LIMIT_BYTES),
        name="attention",
    )(q3, k3, v3, tri)
    return out.reshape(n, sb)


def _mix_out_kernel(*refs, tiles_per_seq, sb, n_experts):
    (attn_ref, cu_ref, halo_ref, b_ref, h_ref, ag_ref, cg_ref, cw_ref, cb_ref,
     wo_ref, fg_ref) = refs[:11]
    if n_experts:
        rw_ref, h1_ref, f_ref, route_ref = refs[11:]
    else:
        h1_ref, f_ref = refs[11:]

    attn_n = _rms(attn_ref[...].astype(F32), ag_ref[...])

    cu = cu_ref[...]
    first = pl.program_id(0) % tiles_per_seq == 0
    halo = jnp.where(first, 0.0, halo_ref[...])
    rowi = lax.broadcasted_iota(jnp.int32, cu.shape, 0)
    prev1 = jnp.where(rowi == 0, halo[HALO_ROWS - 1:HALO_ROWS], pltpu.roll(cu, 1, axis=0))
    prev2 = jnp.where(rowi == 0, halo[HALO_ROWS - 2:HALO_ROWS - 1],
                      jnp.where(rowi == 1, halo[HALO_ROWS - 1:HALO_ROWS],
                                pltpu.roll(cu, 2, axis=0)))
    conv = cw_ref[0:1] * prev2 + cw_ref[1:2] * prev1 + cw_ref[2:3] * cu + cb_ref[...]
    conv_n = _rms(b_ref[...] * conv, cg_ref[...])

    h1 = (h_ref[...] + _dot(attn_n.astype(BF16), wo_ref[0:sb])
          + _dot(conv_n.astype(BF16), wo_ref[sb:]))
    h1_ref[...] = h1
    f = _rms(h1, fg_ref[...])
    f_ref[...] = f.astype(f_ref.dtype)

    if n_experts:
        logits = jnp.dot(f, rw_ref[...], preferred_element_type=F32,
                         precision=lax.Precision.HIGHEST)
        lane = lax.broadcasted_iota(jnp.int32, logits.shape, 1)
        neg = jnp.float32(-jnp.inf)
        logits = jnp.where(lane < n_experts, logits, neg)
        m1 = jnp.max(logits, axis=-1, keepdims=True)
        i1 = jnp.min(jnp.where(logits == m1, lane, LANES), axis=-1, keepdims=True)
        rest = jnp.where(lane == i1, neg, logits)
        m2 = jnp.max(rest, axis=-1, keepdims=True)
        i2 = jnp.min(jnp.where(rest == m2, lane, LANES), axis=-1, keepdims=True)
        e2 = jnp.exp(m2 - m1)
        g1 = 1.0 / (1.0 + e2)
        g2 = e2 * g1
        route_ref[...] = jnp.where(
            lane == 0, i1.astype(F32), jnp.where(
                lane == 1, i2.astype(F32), jnp.where(
                    lane == 2, g1, jnp.where(lane == 3, g2, 0.0))))


def _mix_out(attn, cu, b, h, ag, cg, cw, cb, w_o, fg, router_w, *, tm, seq_len, n_experts):
    n, d = h.shape
    sb = attn.shape[1]
    row = lambda i: (i, 0)
    fixed = lambda i: (0, 0)
    halo = lambda i: (jnp.maximum(i * (tm // HALO_ROWS) - 1, 0), 0)
    in_specs = [
        pl.BlockSpec((tm, sb), row),
        pl.BlockSpec((tm, sb), row),
        pl.BlockSpec((HALO_ROWS, sb), halo),
        pl.BlockSpec((tm, sb), row),
        pl.BlockSpec((tm, d), row),
        pl.BlockSpec((1, sb), fixed),
        pl.BlockSpec((1, sb), fixed),
        pl.BlockSpec((CONV_K, sb), fixed),
        pl.BlockSpec((1, sb), fixed),
        pl.BlockSpec(w_o.shape, fixed),
        pl.BlockSpec((1, d), fixed),
    ]
    args = [attn, cu, cu, b, h, ag, cg, cw, cb, w_o, fg]
    out_specs = [pl.BlockSpec((tm, d), row), pl.BlockSpec((tm, d), row)]
    out_shape = [jax.ShapeDtypeStruct((n, d), F32),
                 jax.ShapeDtypeStruct((n, d), F32 if n_experts else BF16)]
    if n_experts:
        in_specs.append(pl.BlockSpec(router_w.shape, fixed))
        args.append(router_w)
        out_specs.append(pl.BlockSpec((tm, LANES), row))
        out_shape.append(jax.ShapeDtypeStruct((n, LANES), F32))
    return pl.pallas_call(
        functools.partial(_mix_out_kernel, tiles_per_seq=seq_len // tm, sb=sb,
                          n_experts=n_experts),
        grid=(n // tm,),
        in_specs=in_specs,
        out_specs=out_specs,
        out_shape=out_shape,
        compiler_params=pltpu.CompilerParams(
            dimension_semantics=("arbitrary",), vmem_limit_bytes=VMEM_LIMIT_BYTES),
        name="mix_out_moe" if n_experts else "mix_out",
    )(*args)


def _swiglu_step(x, w1_ref, w3_ref, w2_ref):
    a = _dot(x, w1_ref[0])
    hidden = a * jax.nn.sigmoid(a) * _dot(x, w3_ref[0])
    return _dot(hidden.astype(BF16), w2_ref[0])


def _ple(h2, p, pg, wg, wp):
    gate = jax.nn.sigmoid(_dot(_rms(h2, pg).astype(BF16), wg))
    return h2 + gate * _dot(p.astype(BF16), wp)


def _ffn_kernel(f_ref, w1_ref, w3_ref, w2_ref, h1_ref, p_ref, pg_ref, wg_ref, wp_ref,
                o_ref, acc_ref):
    c = pl.program_id(1)

    @pl.when(c == 0)
    def _():
        acc_ref[...] = jnp.zeros_like(acc_ref)

    acc_ref[...] += _swiglu_step(f_ref[...], w1_ref, w3_ref, w2_ref)

    @pl.when(c == pl.num_programs(1) - 1)
    def _():
        o_ref[...] = _ple(h1_ref[...] + acc_ref[...], p_ref[...], pg_ref[...], wg_ref[...],
                          wp_ref[...])


def _ffn(f, w1, w3, w2, h1, p, pg, wg, wp, *, tm, tf):
    n, d = h1.shape
    d_ff = w1.shape[2]
    row = lambda i, c: (i, 0)
    fixed = lambda i, c: (0, 0)
    in_specs = [
        pl.BlockSpec((tm, d), row),
        pl.BlockSpec((1, d, tf), lambda i, c: (0, 0, c)),
        pl.BlockSpec((1, d, tf), lambda i, c: (0, 0, c)),
        pl.BlockSpec((1, tf, d), lambda i, c: (0, c, 0)),
        pl.BlockSpec((tm, d), row),
        pl.BlockSpec((tm, p.shape[1]), row),
        pl.BlockSpec((1, d), fixed),
        pl.BlockSpec(wg.shape, fixed),
        pl.BlockSpec(wp.shape, fixed),
    ]
    return pl.pallas_call(
        _ffn_kernel,
        grid=(n // tm, d_ff // tf),
        in_specs=in_specs,
        out_specs=pl.BlockSpec((tm, d), row),
        out_shape=jax.ShapeDtypeStruct((n, d), F32),
        scratch_shapes=[pltpu.VMEM((tm, d), F32)],
        compiler_params=pltpu.CompilerParams(
            dimension_semantics=("arbitrary", "arbitrary"),
            vmem_limit_bytes=VMEM_LIMIT_BYTES),
        name="ffn_dense",
    )(f, w1, w3, w2, h1, p, pg, wg, wp)


def _route_plan(route, n_experts, tm):
    n = route.shape[0]
    pairs = TOP_K * n
    n_rows = pairs + n_experts * tm
    eid = route[:, 0:TOP_K].astype(jnp.int32).reshape(pairs)
    gates = route[:, TOP_K:2 * TOP_K].reshape(pairs)
    onehot = (eid[:, None] == jnp.arange(n_experts)[None, :]).astype(jnp.int32)
    upto = jnp.cumsum(onehot, axis=0)
    rank = jnp.sum((upto - onehot) * onehot, axis=1)
    padded = (upto[-1] + tm - 1) // tm * tm
    ends = jnp.cumsum(padded)
    pos = (ends - padded)[eid] + rank
    pair_of_row = jnp.full((n_rows,), -1, jnp.int32).at[pos].set(jnp.arange(pairs, dtype=jnp.int32))
    real = pair_of_row >= 0
    safe = jnp.maximum(pair_of_row, 0)
    src_token = jnp.where(real, safe // TOP_K, 0)
    gate_row = jnp.where(real, gates[safe], 0.0)
    n_tiles = n_rows // tm
    tile_rows = jnp.sum(real.reshape(n_tiles, tm), axis=1).astype(jnp.int32)
    tile_expert = jnp.sum(jnp.arange(n_tiles, dtype=jnp.int32)[:, None] * tm >= ends[None, :],
                          axis=1)
    tile_expert = jnp.minimum(tile_expert, n_experts - 1).astype(jnp.int32)
    return (src_token.reshape(n_tiles, 1, tm), safe.reshape(n_tiles, 1, tm),
            gate_row.reshape(n_rows, 1), tile_expert, tile_rows)


def _for_each_row(count, fn, unroll=8):
    if isinstance(count, int):
        full = count // unroll
    else:
        full = lax.shift_right_logical(count, jnp.int32(int(np.log2(unroll))))

    def chunk(i, carry):
        for u in range(unroll):
            fn(i * unroll + u)
        return carry

    def single(r, carry):
        fn(r)
        return carry

    lax.fori_loop(0, full, chunk, 0)
    lax.fori_loop(full * unroll, count, single, 0)


def _moe_ffn_kernel(expert_ref, count_ref, src_ref, dst_ref, f_hbm, gate_ref,
                    w1_ref, w3_ref, w2_ref, y_hbm, rows_ref, x_ref, acc_ref, sems, *, tm):
    j = pl.program_id(0)
    c = pl.program_id(1)
    n_real = count_ref[j]
    active = n_real > 0

    @pl.when(active & (c == 0))
    def _():
        def gather(r):
            pltpu.make_async_copy(f_hbm.at[pl.ds(src_ref[0, 0, r], 1), :],
                                  rows_ref.at[pl.ds(r, 1), :], sems.at[0]).start()
        _for_each_row(tm, gather)
        pltpu.make_async_copy(f_hbm.at[pl.ds(0, tm), :], rows_ref, sems.at[0]).wait()
        x_ref[...] = rows_ref[...].astype(BF16)
        acc_ref[...] = jnp.zeros_like(acc_ref)

    @pl.when(active)
    def _():
        acc_ref[...] += _swiglu_step(x_ref[...], w1_ref, w3_ref, w2_ref)

    @pl.when(active & (c == pl.num_programs(1) - 1))
    def _():
        rows_ref[...] = acc_ref[...] * gate_ref[...]

        def row_copy(r):
            return pltpu.make_async_copy(rows_ref.at[pl.ds(r, 1), :],
                                         y_hbm.at[pl.ds(dst_ref[0, 0, r], 1), :], sems.at[1])
        _for_each_row(n_real, lambda r: row_copy(r).start())
        bulk = pl.multiple_of(n_real // SUBLANES * SUBLANES, SUBLANES)

        @pl.when(bulk > 0)
        def _():
            pltpu.make_async_copy(rows_ref.at[pl.ds(0, bulk), :],
                                  y_hbm.at[pl.ds(0, bulk), :], sems.at[1]).wait()

        lax.fori_loop(bulk, n_real, lambda r, carry: (row_copy(r).wait(), carry)[1], 0)


def _moe_ffn(f, plan, w1, w3, w2, *, tm, tf):
    src_token, dst_row, gate_row, tile_expert, tile_rows = plan
    n, d = f.shape
    n_exp, _, d_ff = w1.shape
    n_tiles = src_token.shape[0]
    n_chunks = d_ff // tf
    chunk = lambda j, c, count: jnp.where(count[j] > 0, c, n_chunks - 1)
    tile = lambda j, c, ex, count: (j, 0, 0)
    grid_spec = pltpu.PrefetchScalarGridSpec(
        num_scalar_prefetch=2,
        grid=(n_tiles, n_chunks),
        in_specs=[
            pl.BlockSpec((1, 1, tm), tile, memory_space=pltpu.SMEM),
            pl.BlockSpec((1, 1, tm), tile, memory_space=pltpu.SMEM),
            pl.BlockSpec(memory_space=pl.ANY),
            pl.BlockSpec((tm, 1), lambda j, c, ex, count: (j, 0)),
            pl.BlockSpec((1, d, tf), lambda j, c, ex, count: (ex[j], 0, chunk(j, c, count))),
            pl.BlockSpec((1, d, tf), lambda j, c, ex, count: (ex[j], 0, chunk(j, c, count))),
            pl.BlockSpec((1, tf, d), lambda j, c, ex, count: (ex[j], chunk(j, c, count), 0)),
        ],
        out_specs=pl.BlockSpec(memory_space=pl.ANY),
        scratch_shapes=[pltpu.VMEM((tm, d), F32), pltpu.VMEM((tm, d), BF16),
                        pltpu.VMEM((tm, d), F32), pltpu.SemaphoreType.DMA((2,))],
    )
    return pl.pallas_call(
        functools.partial(_moe_ffn_kernel, tm=tm),
        grid_spec=grid_spec,
        out_shape=jax.ShapeDtypeStruct((TOP_K * n, d), F32),
        compiler_params=pltpu.CompilerParams(
            dimension_semantics=("arbitrary", "arbitrary"),
            vmem_limit_bytes=VMEM_LIMIT_BYTES),
        name="ffn_moe",
    )(tile_expert, tile_rows, src_token, dst_row, f, gate_row, w1, w3, w2)


def _ple_kernel(h1_ref, y_ref, p_ref, pg_ref, wg_ref, wp_ref, o_ref, *, d):
    y = y_ref[...]
    h2 = h1_ref[...]
    for kk in range(TOP_K):
        h2 = h2 + y[:, kk * d:(kk + 1) * d]
    o_ref[...] = _ple(h2, p_ref[...], pg_ref[...], wg_ref[...], wp_ref[...])


def _ple_combine(h1, y, p, pg, wg, wp, *, tm):
    n, d = h1.shape
    y2 = y.reshape(y.shape[0] // TOP_K, TOP_K * d)
    row = lambda i: (i, 0)
    fixed = lambda i: (0, 0)
    return pl.pallas_call(
        functools.partial(_ple_kernel, d=d),
        grid=(n // tm,),
        in_specs=[
            pl.BlockSpec((tm, d), row),
            pl.BlockSpec((tm, TOP_K * d), row),
            pl.BlockSpec((tm, p.shape[1]), row),
            pl.BlockSpec((1, d), fixed),
            pl.BlockSpec(wg.shape, fixed),
            pl.BlockSpec(wp.shape, fixed),
        ],
        out_specs=pl.BlockSpec((tm, d), row),
        out_shape=jax.ShapeDtypeStruct((n, d), F32),
        compiler_params=pltpu.CompilerParams(
            dimension_semantics=("arbitrary",), vmem_limit_bytes=VMEM_LIMIT_BYTES),
        name="ple_combine",
    )(h1, y2, p, pg, wg, wp)


def kernel(x, p, mix_norm_g, w_in, q_norm_g, k_norm_g, conv_w, conv_b, attn_out_g, conv_out_g,
           w_o, ffn_norm_g, dense_w1, dense_w3, dense_w2, router_w, moe_w1, moe_w3, moe_w2,
           ple_norm_g, ple_gate_w, ple_proj_w):
    bsz, seq_len, d = x.shape
    depth = w_in.shape[0]
    sb = attn_out_g.shape[1]
    n_heads = sb // HEAD_DIM
    n_experts = router_w.shape[2]
    n = bsz * seq_len
    tm = min(ROW_TILE, seq_len)
    t = min(ATTN_TILE, seq_len)
    tm_ffn = min(FFN_ROW_TILE, n)

    head_id = jnp.arange(sb) // HEAD_DIM
    head_mean = jnp.where(head_id[:, None] == head_id[None, :], 1.0 / HEAD_DIM, 0.0).astype(BF16)
    ar = jnp.arange(t)
    tri = (ar[:, None] >= ar[None, :]).astype(BF16)
    q_scale = LOG2E / float(np.sqrt(HEAD_DIM))
    row2 = lambda a: a.reshape(1, -1)

    h = x.reshape(n, d)
    for i in range(depth):
        qg = row2(jnp.tile(q_norm_g[i], n_heads) * q_scale)
        kg = row2(jnp.tile(k_norm_g[i], n_heads))
        q, k, v, cu, b = _mix_in(h, row2(mix_norm_g[i]), w_in[i].astype(BF16), qg, kg,
                                 head_mean, tm=tm)
        attn = _attention(q, k, v, tri, bsz=bsz, t=t, pairs=ATTN_PAIRS)
        j = i // 2
        routed = i % 2 == 1
        rw = None
        if routed:
            rw = jnp.zeros((d, LANES), F32).at[:, :n_experts].set(router_w[j])
        outs = _mix_out(attn, cu, b, h, row2(attn_out_g[i]), row2(conv_out_g[i]), conv_w[i],
                        row2(conv_b[i]), w_o[i].astype(BF16), row2(ffn_norm_g[i]), rw,
                        tm=tm, seq_len=seq_len, n_experts=n_experts if routed else 0)
        pi = p[i].reshape(n, -1)
        ple_w = (row2(ple_norm_g[i]), ple_gate_w[i].astype(BF16), ple_proj_w[i].astype(BF16))
        if routed:
            h1, f, route = outs
            plan = _route_plan(route, n_experts, MOE_ROW_TILE)
            y = _moe_ffn(f, plan, moe_w1[j].astype(BF16), moe_w3[j].astype(BF16),
                         moe_w2[j].astype(BF16), tm=MOE_ROW_TILE, tf=FFN_COL_TILE)
            h = _ple_combine(h1, y, pi, *ple_w, tm=tm)
        else:
            h1, f = outs
            h = _ffn(f, dense_w1[j:j + 1].astype(BF16), dense_w3[j:j + 1].astype(BF16),
                     dense_w2[j:j + 1].astype(BF16), h1, pi, *ple_w, tm=tm_ffn,
                     tf=FFN_COL_TILE)
    return h.reshape(bsz, seq_len, d)
```

```python
import functools

import numpy as np
import jax
import jax.numpy as jnp
from jax import lax
from jax.experimental import pallas as pl
from jax.experimental.pallas import tpu as pltpu

F32 = jnp.float32
BF16 = jnp.bfloat16

EPS = 1e-6
HEAD_DIM = 64
CONV_K = 3
TOP_K = 2
LANES = 128
SUBLANES = 8
HALO_ROWS = SUBLANES
LOG2E = float(np.log2(np.e))

VMEM_LIMIT_BYTES = 56 * 1024 * 1024

ROW_TILE = 512
ATTN_TILE = 256
ATTN_PAIRS = 2
SP_CLAMP = 64.0
DONE_LOG2 = 160.0
FFN_ROW_TILE = 1024
FFN_COL_TILE = 512
MOE_ROW_TILE = 512
MOE_COL_TILE = 896


def _rms(x, g):
    return x * lax.rsqrt(jnp.mean(x * x, axis=-1, keepdims=True) + EPS) * g


def _dot(a, b):
    return jnp.dot(a, b, preferred_element_type=F32)


def _mix_in_kernel(h_ref, g_ref, w_ref, qg_ref, kg_ref, hm_ref,
                   q_ref, k_ref, v_ref, cu_ref, b_ref, *, sb):
    a = _rms(h_ref[...], g_ref[...]).astype(BF16)
    hm = hm_ref[...]

    def qk_norm(t, g):
        ms = _dot((t * t).astype(BF16), hm)
        return (t * lax.rsqrt(ms + EPS) * g).astype(BF16)

    q_ref[...] = qk_norm(_dot(a, w_ref[:, 0:sb]), qg_ref[...])
    k_ref[...] = qk_norm(_dot(a, w_ref[:, sb:2 * sb]), kg_ref[...])
    v_ref[...] = _dot(a, w_ref[:, 2 * sb:3 * sb]).astype(BF16)
    u = _dot(a, w_ref[:, 3 * sb:4 * sb])
    c = _dot(a, w_ref[:, 4 * sb:5 * sb])
    cu_ref[...] = c * u
    b_ref[...] = _dot(a, w_ref[:, 5 * sb:6 * sb])


def _mix_in(h, g, w_in, qg, kg, head_mean, *, tm):
    n, d = h.shape
    sb = head_mean.shape[0]
    row = lambda i: (i, 0)
    fixed = lambda i: (0, 0)
    out_bf = jax.ShapeDtypeStruct((n, sb), BF16)
    out_f32 = jax.ShapeDtypeStruct((n, sb), F32)
    return pl.pallas_call(
        functools.partial(_mix_in_kernel, sb=sb),
        grid=(n // tm,),
        in_specs=[
            pl.BlockSpec((tm, d), row),
            pl.BlockSpec((1, d), fixed),
            pl.BlockSpec(w_in.shape, fixed),
            pl.BlockSpec((1, sb), fixed),
            pl.BlockSpec((1, sb), fixed),
            pl.BlockSpec((sb, sb), fixed),
        ],
        out_specs=[pl.BlockSpec((tm, sb), row)] * 5,
        out_shape=[out_bf, out_bf, out_bf, out_f32, out_f32],
        compiler_params=pltpu.CompilerParams(
            dimension_semantics=("arbitrary",), vmem_limit_bytes=VMEM_LIMIT_BYTES),
        name="mix_in",
    )(h, g, w_in, qg, kg, head_mean)


def _attn_kernel(q_ref, k_ref, v_ref, u_ref, o_ref, acc_ref, c_ref, *, t, pairs):
    i = pl.program_id(2)
    heads = 2 * pairs
    q = q_ref[0]
    lane = lax.broadcasted_iota(jnp.int32, (t, LANES), 1)
    lo_half = lane < HEAD_DIM
    q_heads = []
    for p in range(pairs):
        qp = q[:, p * LANES:(p + 1) * LANES]
        q_heads += [jnp.where(lo_half, qp, jnp.zeros_like(qp)),
                    jnp.where(lo_half, jnp.zeros_like(qp), qp)]
    tri = u_ref[...]
    row = lax.broadcasted_iota(jnp.int32, (t, t), 0)
    col = lax.broadcasted_iota(jnp.int32, (t, t), 1)
    causal = col < row

    acc_ref[...] = jnp.zeros_like(acc_ref)
    c_ref[...] = jnp.zeros_like(c_ref)

    def block(j, diagonal):
        start = pl.multiple_of(j * t, t)
        kj = k_ref[0, pl.ds(start, t), :]
        vj = v_ref[0, pl.ds(start, t), :]
        for hd in range(heads):
            p = hd // 2
            z = lax.dot_general(q_heads[hd], kj[:, p * LANES:(p + 1) * LANES],
                                (((1,), (1,)), ((), ())), preferred_element_type=F32)
            sp = jnp.maximum(z, jnp.log2(1.0 + jnp.exp2(jnp.minimum(z, SP_CLAMP))))
            if diagonal:
                sp = jnp.where(causal, sp, 0.0)
            incl = _dot(sp.astype(BF16), tri)
            w = jnp.exp2(z - incl - c_ref[hd])
            if diagonal:
                w = jnp.where(causal, w, 0.0)
            acc_ref[hd] += _dot(w.astype(BF16), vj[:, p * LANES:(p + 1) * LANES])
            c_ref[hd] += incl[:, 0:1]

    def decay_floor():
        m = jnp.min(c_ref[0])
        for hd in range(1, heads):
            m = jnp.minimum(m, jnp.min(c_ref[hd]))
        return m

    block(i, True)

    def cond(carry):
        j, floor = carry
        return (j >= 0) & (floor < DONE_LOG2)

    def body(carry):
        j, _ = carry
        block(j, False)
        return j - 1, decay_floor()

    lax.while_loop(cond, body, (i - 1, decay_floor()))
    for p in range(pairs):
        o_ref[0, :, p * LANES:(p + 1) * LANES] = jnp.where(
            lo_half, acc_ref[2 * p], acc_ref[2 * p + 1]).astype(o_ref.dtype)


def _attention(q, k, v, tri, *, bsz, t, pairs):
    n, sb = q.shape
    s = n // bsz
    width = LANES * pairs
    q3, k3, v3 = (a.reshape(bsz, s, sb) for a in (q, k, v))
    blk = lambda b, g, i: (b, i, g)
    seq = lambda b, g, i: (b, 0, g)
    out = pl.pallas_call(
        functools.partial(_attn_kernel, t=t, pairs=pairs),
        grid=(bsz, sb // width, s // t),
        in_specs=[
            pl.BlockSpec((1, t, width), blk),
            pl.BlockSpec((1, s, width), seq),
            pl.BlockSpec((1, s, width), seq),
            pl.BlockSpec((t, t), lambda b, g, i: (0, 0)),
        ],
        out_specs=pl.BlockSpec((1, t, width), blk),
        out_shape=jax.ShapeDtypeStruct((bsz, s, sb), BF16),
        scratch_shapes=[pltpu.VMEM((2 * pairs, t, LANES), F32),
                        pltpu.VMEM((2 * pairs, t, 1), F32)],
        compiler_params=pltpu.CompilerParams(
            dimension_semantics=("arbitrary", "arbitrary", "arbitrary"),
            vmem_limit_bytes=VMEM_LIMIT_BYTES),
        name="attention",
    )(q3, k3, v3, tri)
    return out.reshape(n, sb)


def _mix_out_kernel(*refs, tiles_per_seq, sb, n_experts):
    (attn_ref, cu_ref, halo_ref, b_ref, h_ref, ag_ref, cg_ref, cw_ref, cb_ref,
     wo_ref, fg_ref) = refs[:11]
    if n_experts:
        rw_ref, h1_ref, f_ref, route_ref = refs[11:]
    else:
        h1_ref, f_ref = refs[11:]

    attn_n = _rms(attn_ref[...].astype(F32), ag_ref[...])

    cu = cu_ref[...]
    first = pl.program_id(0) % tiles_per_seq == 0
    halo = jnp.where(first, 0.0, halo_ref[...])
    rowi = lax.broadcasted_iota(jnp.int32, cu.shape, 0)
    prev1 = jnp.where(rowi == 0, halo[HALO_ROWS - 1:HALO_ROWS], pltpu.roll(cu, 1, axis=0))
    prev2 = jnp.where(rowi == 0, halo[HALO_ROWS - 2:HALO_ROWS - 1],
                      jnp.where(rowi == 1, halo[HALO_ROWS - 1:HALO_ROWS],
                                pltpu.roll(cu, 2, axis=0)))
    conv = cw_ref[0:1] * prev2 + cw_ref[1:2] * prev1 + cw_ref[2:3] * cu + cb_ref[...]
    conv_n = _rms(b_ref[...] * conv, cg_ref[...])

    h1 = (h_ref[...] + _dot(attn_n.astype(BF16), wo_ref[0:sb])
          + _dot(conv_n.astype(BF16), wo_ref[sb:]))
    h1_ref[...] = h1
    f = _rms(h1, fg_ref[...])
    f_ref[...] = f.astype(f_ref.dtype)

    if n_experts:
        f_hi = f.astype(BF16)
        f_lo = (f - f_hi.astype(F32)).astype(BF16)
        nt = (((1,), (1,)), ((), ()))
        logits = (lax.dot_general(rw_ref[0], f_hi, nt, preferred_element_type=F32)
                  + (lax.dot_general(rw_ref[0], f_lo, nt, preferred_element_type=F32)
                     + lax.dot_general(rw_ref[1], f_hi, nt, preferred_element_type=F32)))
        e_pad = logits.shape[0]
        eid = lax.broadcasted_iota(jnp.int32, logits.shape, 0).astype(F32)
        neg = jnp.float32(-jnp.inf)
        logits = jnp.where(eid < n_experts, logits, neg)
        m1 = jnp.max(logits, axis=0, keepdims=True)
        i1 = jnp.min(jnp.where(logits == m1, eid, float(e_pad)), axis=0, keepdims=True)
        rest = jnp.where(eid == i1, neg, logits)
        m2 = jnp.max(rest, axis=0, keepdims=True)
        i2 = jnp.min(jnp.where(rest == m2, eid, float(e_pad)), axis=0, keepdims=True)
        e2 = jnp.exp(m2 - m1)
        g1 = 1.0 / (1.0 + e2)
        g2 = e2 * g1
        slot = lax.broadcasted_iota(jnp.int32, route_ref.shape, 0)
        route_ref[...] = jnp.where(
            slot == 0, i1, jnp.where(slot == 1, i2, jnp.where(
                slot == 2, g1, jnp.where(slot == 3, g2, 0.0))))


def _mix_out(attn, cu, b, h, ag, cg, cw, cb, w_o, fg, router_w, *, tm, seq_len, n_experts):
    n, d = h.shape
    sb = attn.shape[1]
    row = lambda i: (i, 0)
    fixed = lambda i: (0, 0)
    halo = lambda i: (jnp.maximum(i * (tm // HALO_ROWS) - 1, 0), 0)
    in_specs = [
        pl.BlockSpec((tm, sb), row),
        pl.BlockSpec((tm, sb), row),
        pl.BlockSpec((HALO_ROWS, sb), halo),
        pl.BlockSpec((tm, sb), row),
        pl.BlockSpec((tm, d), row),
        pl.BlockSpec((1, sb), fixed),
        pl.BlockSpec((1, sb), fixed),
        pl.BlockSpec((CONV_K, sb), fixed),
        pl.BlockSpec((1, sb), fixed),
        pl.BlockSpec(w_o.shape, fixed),
        pl.BlockSpec((1, d), fixed),
    ]
    args = [attn, cu, cu, b, h, ag, cg, cw, cb, w_o, fg]
    out_specs = [pl.BlockSpec((tm, d), row), pl.BlockSpec((tm, d), row)]
    out_shape = [jax.ShapeDtypeStruct((n, d), F32),
                 jax.ShapeDtypeStruct((n, d), F32 if n_experts else BF16)]
    if n_experts:
        in_specs.append(pl.BlockSpec(router_w.shape, lambda i: (0, 0, 0)))
        args.append(router_w)
        out_specs.append(pl.BlockSpec((SUBLANES, tm), lambda i: (0, i)))
        out_shape.append(jax.ShapeDtypeStruct((SUBLANES, n), F32))
    return pl.pallas_call(
        functools.partial(_mix_out_kernel, tiles_per_seq=seq_len // tm, sb=sb,
                          n_experts=n_experts),
        grid=(n // tm,),
        in_specs=in_specs,
        out_specs=out_specs,
        out_shape=out_shape,
        compiler_params=pltpu.CompilerParams(
            dimension_semantics=("arbitrary",), vmem_limit_bytes=VMEM_LIMIT_BYTES),
        name="mix_out_moe" if n_experts else "mix_out",
    )(*args)


def _swiglu_step(x, w1_ref, w3_ref, w2_ref):
    a = _dot(x, w1_ref[0])
    hidden = a * jax.nn.sigmoid(a) * _dot(x, w3_ref[0])
    return _dot(hidden.astype(BF16), w2_ref[0])


def _ple(h2, p, pg, wg, wp):
    gate = jax.nn.sigmoid(_dot(_rms(h2, pg).astype(BF16), wg))
    return h2 + gate * _dot(p.astype(BF16), wp)


def _ffn_kernel(f_ref, w1_ref, w3_ref, w2_ref, h1_ref, p_ref, pg_ref, wg_ref, wp_ref,
                o_ref, acc_ref):
    c = pl.program_id(1)

    @pl.when(c == 0)
    def _():
        acc_ref[...] = jnp.zeros_like(acc_ref)

    acc_ref[...] += _swiglu_step(f_ref[...], w1_ref, w3_ref, w2_ref)

    @pl.when(c == pl.num_programs(1) - 1)
    def _():
        o_ref[...] = _ple(h1_ref[...] + acc_ref[...], p_ref[...], pg_ref[...], wg_ref[...],
                          wp_ref[...])


def _ffn(f, w1, w3, w2, h1, p, pg, wg, wp, *, tm, tf):
    n, d = h1.shape
    d_ff = w1.shape[2]
    row = lambda i, c: (i, 0)
    fixed = lambda i, c: (0, 0)
    in_specs = [
        pl.BlockSpec((tm, d), row),
        pl.BlockSpec((1, d, tf), lambda i, c: (0, 0, c)),
        pl.BlockSpec((1, d, tf), lambda i, c: (0, 0, c)),
        pl.BlockSpec((1, tf, d), lambda i, c: (0, c, 0)),
        pl.BlockSpec((tm, d), row),
        pl.BlockSpec((tm, p.shape[1]), row),
        pl.BlockSpec((1, d), fixed),
        pl.BlockSpec(wg.shape, fixed),
        pl.BlockSpec(wp.shape, fixed),
    ]
    return pl.pallas_call(
        _ffn_kernel,
        grid=(n // tm, d_ff // tf),
        in_specs=in_specs,
        out_specs=pl.BlockSpec((tm, d), row),
        out_shape=jax.ShapeDtypeStruct((n, d), F32),
        scratch_shapes=[pltpu.VMEM((tm, d), F32)],
        compiler_params=pltpu.CompilerParams(
            dimension_semantics=("arbitrary", "arbitrary"),
            vmem_limit_bytes=VMEM_LIMIT_BYTES),
        name="ffn_dense",
    )(f, w1, w3, w2, h1, p, pg, wg, wp)


def _route_plan(route, n_experts, tm):
    n = route.shape[1]
    pairs = TOP_K * n
    n_rows = pairs + n_experts * tm
    eid = route[0:TOP_K].T.astype(jnp.int32).reshape(pairs)
    gates = route[TOP_K:2 * TOP_K].T.reshape(pairs)
    onehot = (eid[:, None] == jnp.arange(n_experts)[None, :]).astype(jnp.int32)
    upto = jnp.cumsum(onehot, axis=0)
    rank = jnp.sum((upto - onehot) * onehot, axis=1)
    padded = (upto[-1] + tm - 1) // tm * tm
    ends = jnp.cumsum(padded)
    pos = (ends - padded)[eid] + rank
    pair_of_row = jnp.full((n_rows,), -1, jnp.int32).at[pos].set(jnp.arange(pairs, dtype=jnp.int32))
    real = pair_of_row >= 0
    safe = jnp.maximum(pair_of_row, 0)
    src_token = jnp.where(real, safe // TOP_K, 0)
    gate_row = jnp.where(real, gates[safe], 0.0)
    n_tiles = n_rows // tm
    tile_rows = jnp.sum(real.reshape(n_tiles, tm), axis=1).astype(jnp.int32)
    tile_expert = jnp.sum(jnp.arange(n_tiles, dtype=jnp.int32)[:, None] * tm >= ends[None, :],
                          axis=1)
    tile_expert = jnp.minimum(tile_expert, n_experts - 1).astype(jnp.int32)
    row = jnp.arange(n_rows, dtype=jnp.int32)
    spare = pairs + (row // tm % 2) * tm + row % tm
    dst_row = jnp.where(real, (safe % TOP_K) * n + safe // TOP_K, spare)
    src_token = jnp.concatenate([src_token, jnp.zeros((tm,), jnp.int32)])
    dst_row = jnp.concatenate([pairs + tm + jnp.arange(tm, dtype=jnp.int32), dst_row])
    return (src_token.reshape(n_tiles + 1, 1, tm), dst_row.reshape(n_tiles + 1, 1, tm),
            gate_row.reshape(n_rows, 1), tile_expert, tile_rows)


def _for_each_row(count, fn, unroll=8):
    if isinstance(count, int):
        full = count // unroll
    else:
        full = lax.shift_right_logical(count, jnp.int32(int(np.log2(unroll))))

    def chunk(i, carry):
        for u in range(unroll):
            fn(i * unroll + u)
        return carry

    def single(r, carry):
        fn(r)
        return carry

    lax.fori_loop(0, full, chunk, 0)
    lax.fori_loop(full * unroll, count, single, 0)


def _moe_ffn_kernel(expert_ref, count_ref, src0_ref, src_ref, dst_ref, f_hbm, gate_ref,
                    w1_ref, w3_ref, w2_ref, y_hbm, rows_ref, x_ref, acc_ref, stage_ref,
                    gather_sem, scatter_sem, *, tm, n_chunks, n_pairs):
    j = pl.program_id(0)
    c = pl.program_id(1)
    last_tile = pl.num_programs(0) - 1
    per_step = tm // n_chunks
    active = count_ref[j] > 0
    cur = j % 2
    other = 1 - cur

    def tile_gather(slot):
        return pltpu.make_async_copy(f_hbm.at[pl.ds(0, tm), :], rows_ref.at[slot],
                                     gather_sem.at[slot])

    def tile_scatter(slot):
        return pltpu.make_async_copy(stage_ref.at[slot], y_hbm.at[pl.ds(0, tm), :],
                                     scatter_sem.at[slot])

    @pl.when((j == 0) & (c == 0))
    def _():
        stage_ref[...] = jnp.zeros_like(stage_ref)
        for slot in range(2):
            spare = pltpu.make_async_copy(stage_ref.at[slot],
                                          y_hbm.at[pl.ds(n_pairs + slot * tm, tm), :],
                                          scatter_sem.at[slot])
            spare.start()
            spare.wait()

        def gather_first(r):
            pltpu.make_async_copy(f_hbm.at[pl.ds(src0_ref[0, 0, r], 1), :],
                                  rows_ref.at[0, pl.ds(r, 1), :], gather_sem.at[0]).start()
        _for_each_row(tm, gather_first)

    @pl.when(c == 0)
    def _():
        tile_gather(cur).wait()

        @pl.when(j >= 1)
        def _():
            tile_scatter(cur).wait()

        x_ref[...] = rows_ref[cur].astype(BF16)
        acc_ref[...] = jnp.zeros_like(acc_ref)

    def start_row_copies():
        for r in range(per_step):
            row = c * per_step + r
            pltpu.make_async_copy(f_hbm.at[pl.ds(src_ref[0, 0, row], 1), :],
                                  rows_ref.at[other, pl.ds(row, 1), :],
                                  gather_sem.at[other]).start()
            pltpu.make_async_copy(stage_ref.at[other, pl.ds(row, 1), :],
                                  y_hbm.at[pl.ds(dst_ref[0, 0, row], 1), :],
                                  scatter_sem.at[other]).start()

    @pl.when(active)
    def _():
        acc_ref[...] += _swiglu_step(x_ref[...], w1_ref, w3_ref, w2_ref)
        start_row_copies()

    @pl.when(jnp.logical_not(active))
    def _():
        start_row_copies()

    @pl.when(active & (c == n_chunks - 1))
    def _():
        stage_ref[cur] = acc_ref[...] * gate_ref[...]

    @pl.when((j == last_tile) & (c == n_chunks - 1))
    def _():
        tile_gather(other).wait()
        tile_scatter(other).wait()


def _moe_ffn(f, plan, w1, w3, w2, *, tm, tf):
    src_token, dst_row, gate_row, tile_expert, tile_rows = plan
    n, d = f.shape
    n_exp, _, d_ff = w1.shape
    n_tiles = tile_rows.shape[0]
    n_chunks = d_ff // tf
    n_pairs = TOP_K * n
    assert n_pairs % tm == 0 and tm % n_chunks == 0 and n_exp < tm
    chunk = lambda j, c, count: jnp.where(count[j] > 0, c, n_chunks - 1)
    grid_spec = pltpu.PrefetchScalarGridSpec(
        num_scalar_prefetch=2,
        grid=(n_tiles, n_chunks),
        in_specs=[
            pl.BlockSpec((1, 1, tm), lambda j, c, ex, count: (0, 0, 0), memory_space=pltpu.SMEM),
            pl.BlockSpec((1, 1, tm), lambda j, c, ex, count: (j + 1, 0, 0),
                         memory_space=pltpu.SMEM),
            pl.BlockSpec((1, 1, tm), lambda j, c, ex, count: (j, 0, 0), memory_space=pltpu.SMEM),
            pl.BlockSpec(memory_space=pl.ANY),
            pl.BlockSpec((tm, 1), lambda j, c, ex, count: (j, 0)),
            pl.BlockSpec((1, d, tf), lambda j, c, ex, count: (ex[j], 0, chunk(j, c, count))),
            pl.BlockSpec((1, d, tf), lambda j, c, ex, count: (ex[j], 0, chunk(j, c, count))),
            pl.BlockSpec((1, tf, d), lambda j, c, ex, count: (ex[j], chunk(j, c, count), 0)),
        ],
        out_specs=pl.BlockSpec(memory_space=pl.ANY),
        scratch_shapes=[pltpu.VMEM((2, tm, d), F32), pltpu.VMEM((tm, d), BF16),
                        pltpu.VMEM((tm, d), F32), pltpu.VMEM((2, tm, d), F32),
                        pltpu.SemaphoreType.DMA((2,)), pltpu.SemaphoreType.DMA((2,))],
    )
    return pl.pallas_call(
        functools.partial(_moe_ffn_kernel, tm=tm, n_chunks=n_chunks, n_pairs=n_pairs),
        grid_spec=grid_spec,
        out_shape=jax.ShapeDtypeStruct((n_pairs + 2 * tm, d), F32),
        compiler_params=pltpu.CompilerParams(
            dimension_semantics=("arbitrary", "arbitrary"),
            vmem_limit_bytes=VMEM_LIMIT_BYTES),
        name="ffn_moe",
    )(tile_expert, tile_rows, src_token, src_token, dst_row, f, gate_row, w1, w3, w2)


def _ple_kernel(h1_ref, y0_ref, y1_ref, p_ref, pg_ref, wg_ref, wp_ref, o_ref):
    h2 = h1_ref[...] + y0_ref[...] + y1_ref[...]
    o_ref[...] = _ple(h2, p_ref[...], pg_ref[...], wg_ref[...], wp_ref[...])


def _ple_combine(h1, y, p, pg, wg, wp, *, tm):
    n, d = h1.shape
    row = lambda i: (i, 0)
    fixed = lambda i: (0, 0)
    return pl.pallas_call(
        _ple_kernel,
        grid=(n // tm,),
        in_specs=[
            pl.BlockSpec((tm, d), row),
            pl.BlockSpec((tm, d), row),
            pl.BlockSpec((tm, d), lambda i: (i + n // tm, 0)),
            pl.BlockSpec((tm, p.shape[1]), row),
            pl.BlockSpec((1, d), fixed),
            pl.BlockSpec(wg.shape, fixed),
            pl.BlockSpec(wp.shape, fixed),
        ],
        out_specs=pl.BlockSpec((tm, d), row),
        out_shape=jax.ShapeDtypeStruct((n, d), F32),
        compiler_params=pltpu.CompilerParams(
            dimension_semantics=("arbitrary",), vmem_limit_bytes=VMEM_LIMIT_BYTES),
        name="ple_combine",
    )(h1, y, y, p, pg, wg, wp)


def kernel(x, p, mix_norm_g, w_in, q_norm_g, k_norm_g, conv_w, conv_b, attn_out_g, conv_out_g,
           w_o, ffn_norm_g, dense_w1, dense_w3, dense_w2, router_w, moe_w1, moe_w3, moe_w2,
           ple_norm_g, ple_gate_w, ple_proj_w):
    bsz, seq_len, d = x.shape
    depth = w_in.shape[0]
    sb = attn_out_g.shape[1]
    n_heads = sb // HEAD_DIM
    n_experts = router_w.shape[2]
    n = bsz * seq_len
    tm = min(ROW_TILE, seq_len)
    t = min(ATTN_TILE, seq_len)
    tm_ffn = min(FFN_ROW_TILE, n)

    head_id = jnp.arange(sb) // HEAD_DIM
    head_mean = jnp.where(head_id[:, None] == head_id[None, :], 1.0 / HEAD_DIM, 0.0).astype(BF16)
    ar = jnp.arange(t)
    tri = (ar[:, None] >= ar[None, :]).astype(BF16)
    q_scale = LOG2E / float(np.sqrt(HEAD_DIM))
    row2 = lambda a: a.reshape(1, -1)

    h = x.reshape(n, d)
    for i in range(depth):
        qg = row2(jnp.tile(q_norm_g[i], n_heads) * q_scale)
        kg = row2(jnp.tile(k_norm_g[i], n_heads))
        q, k, v, cu, b = _mix_in(h, row2(mix_norm_g[i]), w_in[i].astype(BF16), qg, kg,
                                 head_mean, tm=tm)
        attn = _attention(q, k, v, tri, bsz=bsz, t=t, pairs=ATTN_PAIRS)
        j = i // 2
        routed = i % 2 == 1
        rw = None
        if routed:
            e_pad = -(-n_experts // SUBLANES) * SUBLANES
            rw = jnp.pad(router_w[j].T, ((0, e_pad - n_experts), (0, 0)))
            rw_hi = rw.astype(BF16)
            rw = jnp.stack([rw_hi, (rw - rw_hi.astype(F32)).astype(BF16)])
        outs = _mix_out(attn, cu, b, h, row2(attn_out_g[i]), row2(conv_out_g[i]), conv_w[i],
                        row2(conv_b[i]), w_o[i].astype(BF16), row2(ffn_norm_g[i]), rw,
                        tm=tm, seq_len=seq_len, n_experts=n_experts if routed else 0)
        pi = p[i].reshape(n, -1)
        ple_w = (row2(ple_norm_g[i]), ple_gate_w[i].astype(BF16), ple_proj_w[i].astype(BF16))
        if routed:
            h1, f, route = outs
            plan = _route_plan(route, n_experts, MOE_ROW_TILE)
            y = _moe_ffn(f, plan, moe_w1[j].astype(BF16), moe_w3[j].astype(BF16),
                         moe_w2[j].astype(BF16), tm=MOE_ROW_TILE, tf=MOE_COL_TILE)
            h = _ple_combine(h1, y, pi, *ple_w, tm=tm)
        else:
            h1, f = outs
            h = _ffn(f, dense_w1[j:j + 1].astype(BF16), dense_w3[j:j + 1].astype(BF16),
                     dense_w2[j:j + 1].astype(BF16), h1, pi, *ple_w, tm=tm_ffn,
                     tf=FFN_COL_TILE)
    return h.reshape(bsz, seq_len, d)
```

```python
import functools

import numpy as np
import jax
import jax.numpy as jnp
from jax import lax
from jax.experimental import pallas as pl
from jax.experimental.pallas import tpu as pltpu

F32 = jnp.float32
BF16 = jnp.bfloat16

EPS = 1e-6
HEAD_DIM = 64
CONV_K = 3
TOP_K = 2
LANES = 128
SUBLANES = 8
HALO_ROWS = SUBLANES
LOG2E = float(np.log2(np.e))

VMEM_LIMIT_BYTES = 56 * 1024 * 1024

ROW_TILE = 512
ATTN_TILE = 256
ATTN_PAIRS = 2
SP_CLAMP = 64.0
DONE_LOG2 = 160.0
NO_BLOCK_DECAY = 1e30
FFN_ROW_TILE = 1024
FFN_COL_TILE = 512
MOE_ROW_TILE = 512
MOE_COL_TILE = 896


def _rms(x, g):
    return x * lax.rsqrt(jnp.mean(x * x, axis=-1, keepdims=True) + EPS) * g


def _dot(a, b):
    return jnp.dot(a, b, preferred_element_type=F32)


def _mix_in_kernel(h_ref, g_ref, w_ref, qg_ref, kg_ref, hm_ref,
                   q_ref, k_ref, v_ref, cu_ref, b_ref, *, sb):
    a = _rms(h_ref[...], g_ref[...]).astype(BF16)
    hm = hm_ref[...]

    def qk_norm(t, g):
        ms = _dot((t * t).astype(BF16), hm)
        return (t * lax.rsqrt(ms + EPS) * g).astype(BF16)

    q_ref[...] = qk_norm(_dot(a, w_ref[:, 0:sb]), qg_ref[...])
    k_ref[...] = qk_norm(_dot(a, w_ref[:, sb:2 * sb]), kg_ref[...])
    v_ref[...] = _dot(a, w_ref[:, 2 * sb:3 * sb]).astype(BF16)
    u = _dot(a, w_ref[:, 3 * sb:4 * sb])
    c = _dot(a, w_ref[:, 4 * sb:5 * sb])
    cu_ref[...] = c * u
    b_ref[...] = _dot(a, w_ref[:, 5 * sb:6 * sb])


def _mix_in(h, g, w_in, qg, kg, head_mean, *, tm):
    n, d = h.shape
    sb = head_mean.shape[0]
    row = lambda i: (i, 0)
    fixed = lambda i: (0, 0)
    out_bf = jax.ShapeDtypeStruct((n, sb), BF16)
    out_f32 = jax.ShapeDtypeStruct((n, sb), F32)
    return pl.pallas_call(
        functools.partial(_mix_in_kernel, sb=sb),
        grid=(n // tm,),
        in_specs=[
            pl.BlockSpec((tm, d), row),
            pl.BlockSpec((1, d), fixed),
            pl.BlockSpec(w_in.shape, fixed),
            pl.BlockSpec((1, sb), fixed),
            pl.BlockSpec((1, sb), fixed),
            pl.BlockSpec((sb, sb), fixed),
        ],
        out_specs=[pl.BlockSpec((tm, sb), row)] * 5,
        out_shape=[out_bf, out_bf, out_bf, out_f32, out_f32],
        compiler_params=pltpu.CompilerParams(
            dimension_semantics=("arbitrary",), vmem_limit_bytes=VMEM_LIMIT_BYTES),
        name="mix_in",
    )(h, g, w_in, qg, kg, head_mean)


def _attn_kernel(q_ref, k_ref, v_ref, u_ref, o_ref, acc_ref, c_ref, *, t, pairs):
    i = pl.program_id(2)
    heads = 2 * pairs
    q = q_ref[0]
    lane = lax.broadcasted_iota(jnp.int32, (t, LANES), 1)
    lo_half = lane < HEAD_DIM
    q_heads = []
    for p in range(pairs):
        qp = q[:, p * LANES:(p + 1) * LANES]
        q_heads += [jnp.where(lo_half, qp, jnp.zeros_like(qp)),
                    jnp.where(lo_half, jnp.zeros_like(qp), qp)]
    tri = u_ref[...]
    row = lax.broadcasted_iota(jnp.int32, (t, t), 0)
    col = lax.broadcasted_iota(jnp.int32, (t, t), 1)
    causal = col < row

    acc_ref[...] = jnp.zeros_like(acc_ref)
    c_ref[...] = jnp.zeros_like(c_ref)

    def block(j, diagonal, extra_decay=0.0):
        start = pl.multiple_of(j * t, t)
        kj = k_ref[0, pl.ds(start, t), :]
        vj = v_ref[0, pl.ds(start, t), :]
        for hd in range(heads):
            p = hd // 2
            z = lax.dot_general(q_heads[hd], kj[:, p * LANES:(p + 1) * LANES],
                                (((1,), (1,)), ((), ())), preferred_element_type=F32)
            sp = jnp.maximum(z, jnp.log2(1.0 + jnp.exp2(jnp.minimum(z, SP_CLAMP))))
            if diagonal:
                sp = jnp.where(causal, sp, 0.0)
            incl = _dot(sp.astype(BF16), tri)
            w = jnp.exp2(z - incl - (c_ref[hd] + extra_decay))
            if diagonal:
                w = jnp.where(causal, w, 0.0)
            acc_ref[hd] += _dot(w.astype(BF16), vj[:, p * LANES:(p + 1) * LANES])
            c_ref[hd] += incl[:, 0:1]

    def decay_floor():
        m = jnp.min(c_ref[0])
        for hd in range(1, heads):
            m = jnp.minimum(m, jnp.min(c_ref[hd]))
        return m

    block(i, True)
    block(jnp.maximum(i - 1, 0), False, extra_decay=jnp.where(i > 0, 0.0, NO_BLOCK_DECAY))

    def cond(carry):
        j, floor = carry
        return (j >= 0) & (floor < DONE_LOG2)

    def body(carry):
        j, _ = carry
        block(j, False)
        return j - 1, decay_floor()

    lax.while_loop(cond, body, (i - 2, decay_floor()))
    for p in range(pairs):
        o_ref[0, :, p * LANES:(p + 1) * LANES] = jnp.where(
            lo_half, acc_ref[2 * p], acc_ref[2 * p + 1]).astype(o_ref.dtype)


def _attention(q, k, v, tri, *, bsz, t, pairs):
    n, sb = q.shape
    s = n // bsz
    width = LANES * pairs
    q3, k3, v3 = (a.reshape(bsz, s, sb) for a in (q, k, v))
    blk = lambda b, g, i: (b, i, g)
    seq = lambda b, g, i: (b, 0, g)
    out = pl.pallas_call(
        functools.partial(_attn_kernel, t=t, pairs=pairs),
        grid=(bsz, sb // width, s // t),
        in_specs=[
            pl.BlockSpec((1, t, width), blk),
            pl.BlockSpec((1, s, width), seq),
            pl.BlockSpec((1, s, width), seq),
            pl.BlockSpec((t, t), lambda b, g, i: (0, 0)),
        ],
        out_specs=pl.BlockSpec((1, t, width), blk),
        out_shape=jax.ShapeDtypeStruct((bsz, s, sb), BF16),
        scratch_shapes=[pltpu.VMEM((2 * pairs, t, LANES), F32),
                        pltpu.VMEM((2 * pairs, t, 1), F32)],
        compiler_params=pltpu.CompilerParams(
            dimension_semantics=("arbitrary", "arbitrary", "arbitrary"),
            vmem_limit_bytes=VMEM_LIMIT_BYTES),
        name="attention",
    )(q3, k3, v3, tri)
    return out.reshape(n, sb)


def _mix_out_kernel(*refs, tiles_per_seq, sb, n_experts):
    (attn_ref, cu_ref, halo_ref, b_ref, h_ref, ag_ref, cg_ref, cw_ref, cb_ref,
     wo_ref, fg_ref) = refs[:11]
    if n_experts:
        rw_ref, h1_ref, f_ref, route_ref = refs[11:]
    else:
        h1_ref, f_ref = refs[11:]

    attn_n = _rms(attn_ref[...].astype(F32), ag_ref[...])

    cu = cu_ref[...]
    first = pl.program_id(0) % tiles_per_seq == 0
    halo = jnp.where(first, 0.0, halo_ref[...])
    rowi = lax.broadcasted_iota(jnp.int32, cu.shape, 0)
    prev1 = jnp.where(rowi == 0, halo[HALO_ROWS - 1:HALO_ROWS], pltpu.roll(cu, 1, axis=0))
    prev2 = jnp.where(rowi == 0, halo[HALO_ROWS - 2:HALO_ROWS - 1],
                      jnp.where(rowi == 1, halo[HALO_ROWS - 1:HALO_ROWS],
                                pltpu.roll(cu, 2, axis=0)))
    conv = cw_ref[0:1] * prev2 + cw_ref[1:2] * prev1 + cw_ref[2:3] * cu + cb_ref[...]
    conv_n = _rms(b_ref[...] * conv, cg_ref[...])

    h1 = (h_ref[...] + _dot(attn_n.astype(BF16), wo_ref[0:sb])
          + _dot(conv_n.astype(BF16), wo_ref[sb:]))
    h1_ref[...] = h1
    f = _rms(h1, fg_ref[...])
    f_ref[...] = f.astype(f_ref.dtype)

    if n_experts:
        f_hi = f.astype(BF16)
        f_lo = (f - f_hi.astype(F32)).astype(BF16)
        nt = (((1,), (1,)), ((), ()))
        logits = (lax.dot_general(rw_ref[0], f_hi, nt, preferred_element_type=F32)
                  + (lax.dot_general(rw_ref[0], f_lo, nt, preferred_element_type=F32)
                     + lax.dot_general(rw_ref[1], f_hi, nt, preferred_element_type=F32)))
        e_pad = logits.shape[0]
        eid = lax.broadcasted_iota(jnp.int32, logits.shape, 0).astype(F32)
        neg = jnp.float32(-jnp.inf)
        logits = jnp.where(eid < n_experts, logits, neg)
        m1 = jnp.max(logits, axis=0, keepdims=True)
        i1 = jnp.min(jnp.where(logits == m1, eid, float(e_pad)), axis=0, keepdims=True)
        rest = jnp.where(eid == i1, neg, logits)
        m2 = jnp.max(rest, axis=0, keepdims=True)
        i2 = jnp.min(jnp.where(rest == m2, eid, float(e_pad)), axis=0, keepdims=True)
        e2 = jnp.exp(m2 - m1)
        g1 = 1.0 / (1.0 + e2)
        g2 = e2 * g1
        slot = lax.broadcasted_iota(jnp.int32, route_ref.shape, 0)
        route_ref[...] = jnp.where(
            slot == 0, i1, jnp.where(slot == 1, i2, jnp.where(
                slot == 2, g1, jnp.where(slot == 3, g2, 0.0))))


def _mix_out(attn, cu, b, h, ag, cg, cw, cb, w_o, fg, router_w, *, tm, seq_len, n_experts):
    n, d = h.shape
    sb = attn.shape[1]
    row = lambda i: (i, 0)
    fixed = lambda i: (0, 0)
    halo = lambda i: (jnp.maximum(i * (tm // HALO_ROWS) - 1, 0), 0)
    in_specs = [
        pl.BlockSpec((tm, sb), row),
        pl.BlockSpec((tm, sb), row),
        pl.BlockSpec((HALO_ROWS, sb), halo),
        pl.BlockSpec((tm, sb), row),
        pl.BlockSpec((tm, d), row),
        pl.BlockSpec((1, sb), fixed),
        pl.BlockSpec((1, sb), fixed),
        pl.BlockSpec((CONV_K, sb), fixed),
        pl.BlockSpec((1, sb), fixed),
        pl.BlockSpec(w_o.shape, fixed),
        pl.BlockSpec((1, d), fixed),
    ]
    args = [attn, cu, cu, b, h, ag, cg, cw, cb, w_o, fg]
    out_specs = [pl.BlockSpec((tm, d), row), pl.BlockSpec((tm, d), row)]
    out_shape = [jax.ShapeDtypeStruct((n, d), F32),
                 jax.ShapeDtypeStruct((n, d), F32 if n_experts else BF16)]
    if n_experts:
        in_specs.append(pl.BlockSpec(router_w.shape, lambda i: (0, 0, 0)))
        args.append(router_w)
        out_specs.append(pl.BlockSpec((SUBLANES, tm), lambda i: (0, i)))
        out_shape.append(jax.ShapeDtypeStruct((SUBLANES, n), F32))
    return pl.pallas_call(
        functools.partial(_mix_out_kernel, tiles_per_seq=seq_len // tm, sb=sb,
                          n_experts=n_experts),
        grid=(n // tm,),
        in_specs=in_specs,
        out_specs=out_specs,
        out_shape=out_shape,
        compiler_params=pltpu.CompilerParams(
            dimension_semantics=("arbitrary",), vmem_limit_bytes=VMEM_LIMIT_BYTES),
        name="mix_out_moe" if n_experts else "mix_out",
    )(*args)


def _swiglu_step(x, w1_ref, w3_ref, w2_ref):
    a = _dot(x, w1_ref[0])
    hidden = a * jax.nn.sigmoid(a) * _dot(x, w3_ref[0])
    return _dot(hidden.astype(BF16), w2_ref[0])


def _ple(h2, p, pg, wg, wp):
    gate = jax.nn.sigmoid(_dot(_rms(h2, pg).astype(BF16), wg))
    return h2 + gate * _dot(p.astype(BF16), wp)


def _ffn_kernel(f_ref, w1_ref, w3_ref, w2_ref, h1_ref, p_ref, pg_ref, wg_ref, wp_ref,
                o_ref, acc_ref):
    c = pl.program_id(1)

    @pl.when(c == 0)
    def _():
        acc_ref[...] = jnp.zeros_like(acc_ref)

    acc_ref[...] += _swiglu_step(f_ref[...], w1_ref, w3_ref, w2_ref)

    @pl.when(c == pl.num_programs(1) - 1)
    def _():
        o_ref[...] = _ple(h1_ref[...] + acc_ref[...], p_ref[...], pg_ref[...], wg_ref[...],
                          wp_ref[...])


def _ffn(f, w1, w3, w2, h1, p, pg, wg, wp, *, tm, tf):
    n, d = h1.shape
    d_ff = w1.shape[2]
    row = lambda i, c: (i, 0)
    fixed = lambda i, c: (0, 0)
    in_specs = [
        pl.BlockSpec((tm, d), row),
        pl.BlockSpec((1, d, tf), lambda i, c: (0, 0, c)),
        pl.BlockSpec((1, d, tf), lambda i, c: (0, 0, c)),
        pl.BlockSpec((1, tf, d), lambda i, c: (0, c, 0)),
        pl.BlockSpec((tm, d), row),
        pl.BlockSpec((tm, p.shape[1]), row),
        pl.BlockSpec((1, d), fixed),
        pl.BlockSpec(wg.shape, fixed),
        pl.BlockSpec(wp.shape, fixed),
    ]
    return pl.pallas_call(
        _ffn_kernel,
        grid=(n // tm, d_ff // tf),
        in_specs=in_specs,
        out_specs=pl.BlockSpec((tm, d), row),
        out_shape=jax.ShapeDtypeStruct((n, d), F32),
        scratch_shapes=[pltpu.VMEM((tm, d), F32)],
        compiler_params=pltpu.CompilerParams(
            dimension_semantics=("arbitrary", "arbitrary"),
            vmem_limit_bytes=VMEM_LIMIT_BYTES),
        name="ffn_dense",
    )(f, w1, w3, w2, h1, p, pg, wg, wp)


def _route_plan(route, n_experts, tm):
    n = route.shape[1]
    pairs = TOP_K * n
    n_rows = pairs + n_experts * tm
    eid = route[0:TOP_K].T.astype(jnp.int32).reshape(pairs)
    gates = route[TOP_K:2 * TOP_K].T.reshape(pairs)
    pair_bits = int(np.ceil(np.log2(pairs)))
    keys = jnp.sort(eid * (1 << pair_bits) + jnp.arange(pairs, dtype=jnp.int32))
    order = keys & ((1 << pair_bits) - 1)
    counts = jnp.sum(eid[:, None] == jnp.arange(n_experts)[None, :], axis=0).astype(jnp.int32)
    padded = (counts + tm - 1) // tm * tm
    ends = jnp.cumsum(padded)
    first = jnp.cumsum(counts) - counts
    n_tiles = n_rows // tm
    tile_expert = jnp.sum(jnp.arange(n_tiles, dtype=jnp.int32)[:, None] * tm >= ends[None, :],
                          axis=1)
    tile_expert = jnp.minimum(tile_expert, n_experts - 1).astype(jnp.int32)
    row = jnp.arange(n_rows, dtype=jnp.int32)
    row_expert = jnp.repeat(tile_expert, tm)
    rank = row - (ends - padded)[row_expert]
    real = rank < counts[row_expert]
    safe = order[jnp.clip(first[row_expert] + rank, 0, pairs - 1)]
    src_token = jnp.where(real, safe // TOP_K, 0)
    gate_row = jnp.where(real, gates[safe], 0.0)
    tile_rows = jnp.sum(real.reshape(n_tiles, tm), axis=1).astype(jnp.int32)
    spare = pairs + (row // tm % 2) * tm + row % tm
    dst_row = jnp.where(real, (safe % TOP_K) * n + safe // TOP_K, spare)
    src_token = jnp.concatenate([src_token, jnp.zeros((2 * tm,), jnp.int32)])
    dst_row = jnp.concatenate([pairs + tm + jnp.arange(tm, dtype=jnp.int32), dst_row])
    return (src_token.reshape(n_tiles + 2, 1, tm), dst_row.reshape(n_tiles + 1, 1, tm),
            gate_row.reshape(n_rows, 1), tile_expert, tile_rows)


def _for_each_row(count, fn, unroll=8):
    if isinstance(count, int):
        full = count // unroll
    else:
        full = lax.shift_right_logical(count, jnp.int32(int(np.log2(unroll))))

    def chunk(i, carry):
        for u in range(unroll):
            fn(i * unroll + u)
        return carry

    def single(r, carry):
        fn(r)
        return carry

    lax.fori_loop(0, full, chunk, 0)
    lax.fori_loop(full * unroll, count, single, 0)


def _moe_ffn_kernel(expert_ref, count_ref, src0_ref, src_ref, dst_ref, f_hbm, gate_ref,
                    w1_ref, w3_ref, w2_ref, y_hbm, rows_ref, x_ref, acc_ref, stage_ref,
                    gather_sem, scatter_sem, *, tm, n_chunks, n_pairs):
    j = pl.program_id(0)
    c = pl.program_id(1)
    last_tile = pl.num_programs(0) - 1
    last_chunk = n_chunks - 1
    per_step = tm // n_chunks
    active = count_ref[j] > 0
    cur = j % 2
    other = 1 - cur

    def tile_gather(slot, rows=tm):
        return pltpu.make_async_copy(f_hbm.at[pl.ds(0, rows), :],
                                     rows_ref.at[slot, pl.ds(0, rows), :], gather_sem.at[slot])

    def tile_scatter(slot):
        return pltpu.make_async_copy(stage_ref.at[slot], y_hbm.at[pl.ds(0, tm), :],
                                     scatter_sem.at[slot])

    @pl.when((j == 0) & (c == 0))
    def _():
        stage_ref[...] = jnp.zeros_like(stage_ref)
        for slot in range(2):
            spare = pltpu.make_async_copy(stage_ref.at[slot],
                                          y_hbm.at[pl.ds(n_pairs + slot * tm, tm), :],
                                          scatter_sem.at[slot])
            spare.start()
            spare.wait()

        def gather_first(r):
            pltpu.make_async_copy(f_hbm.at[pl.ds(src0_ref[0, 0, r], 1), :],
                                  rows_ref.at[0, pl.ds(r, 1), :], gather_sem.at[0]).start()
        _for_each_row(tm, gather_first)

        def gather_second(r):
            pltpu.make_async_copy(f_hbm.at[pl.ds(src_ref[0, 0, r], 1), :],
                                  rows_ref.at[1, pl.ds(r, 1), :], gather_sem.at[1]).start()
        _for_each_row(per_step, gather_second)

    @pl.when(c == 0)
    def _():
        tile_gather(cur).wait()

        @pl.when(j >= 1)
        def _():
            tile_scatter(cur).wait()

        x_ref[...] = rows_ref[cur].astype(BF16)
        acc_ref[...] = jnp.zeros_like(acc_ref)

    def start_row_copies():
        gather_slot = jnp.where(c == last_chunk, cur, other)
        gather_base = jnp.where(c == last_chunk, 0, (c + 1) * per_step)
        scatter_base = c * per_step
        for r in range(per_step):
            pltpu.make_async_copy(f_hbm.at[pl.ds(src_ref[0, 0, gather_base + r], 1), :],
                                  rows_ref.at[gather_slot, pl.ds(gather_base + r, 1), :],
                                  gather_sem.at[gather_slot]).start()
            pltpu.make_async_copy(stage_ref.at[other, pl.ds(scatter_base + r, 1), :],
                                  y_hbm.at[pl.ds(dst_ref[0, 0, scatter_base + r], 1), :],
                                  scatter_sem.at[other]).start()

    @pl.when(active)
    def _():
        acc_ref[...] += _swiglu_step(x_ref[...], w1_ref, w3_ref, w2_ref)
        start_row_copies()

    @pl.when(jnp.logical_not(active))
    def _():
        start_row_copies()

    @pl.when(active & (c == last_chunk))
    def _():
        stage_ref[cur] = acc_ref[...] * gate_ref[...]

    @pl.when((j == last_tile) & (c == last_chunk))
    def _():
        tile_gather(other).wait()
        tile_gather(cur, per_step).wait()
        tile_scatter(other).wait()


def _moe_ffn(f, plan, w1, w3, w2, *, tm, tf):
    src_token, dst_row, gate_row, tile_expert, tile_rows = plan
    n, d = f.shape
    n_exp, _, d_ff = w1.shape
    n_tiles = tile_rows.shape[0]
    n_chunks = d_ff // tf
    n_pairs = TOP_K * n
    assert n_pairs % tm == 0 and tm % n_chunks == 0 and n_exp < tm
    chunk = lambda j, c, count: jnp.where(count[j] > 0, c, n_chunks - 1)
    grid_spec = pltpu.PrefetchScalarGridSpec(
        num_scalar_prefetch=2,
        grid=(n_tiles, n_chunks),
        in_specs=[
            pl.BlockSpec((1, 1, tm), lambda j, c, ex, count: (0, 0, 0), memory_space=pltpu.SMEM),
            pl.BlockSpec((1, 1, tm),
                         lambda j, c, ex, count: (jnp.where(c == n_chunks - 1, j + 2, j + 1), 0, 0),
                         memory_space=pltpu.SMEM),
            pl.BlockSpec((1, 1, tm), lambda j, c, ex, count: (j, 0, 0), memory_space=pltpu.SMEM),
            pl.BlockSpec(memory_space=pl.ANY),
            pl.BlockSpec((tm, 1), lambda j, c, ex, count: (j, 0)),
            pl.BlockSpec((1, d, tf), lambda j, c, ex, count: (ex[j], 0, chunk(j, c, count))),
            pl.BlockSpec((1, d, tf), lambda j, c, ex, count: (ex[j], 0, chunk(j, c, count))),
            pl.BlockSpec((1, tf, d), lambda j, c, ex, count: (ex[j], chunk(j, c, count), 0)),
        ],
        out_specs=pl.BlockSpec(memory_space=pl.ANY),
        scratch_shapes=[pltpu.VMEM((2, tm, d), F32), pltpu.VMEM((tm, d), BF16),
                        pltpu.VMEM((tm, d), F32), pltpu.VMEM((2, tm, d), F32),
                        pltpu.SemaphoreType.DMA((2,)), pltpu.SemaphoreType.DMA((2,))],
    )
    return pl.pallas_call(
        functools.partial(_moe_ffn_kernel, tm=tm, n_chunks=n_chunks, n_pairs=n_pairs),
        grid_spec=grid_spec,
        out_shape=jax.ShapeDtypeStruct((n_pairs + 2 * tm, d), F32),
        compiler_params=pltpu.CompilerParams(
            dimension_semantics=("arbitrary", "arbitrary"),
            vmem_limit_bytes=VMEM_LIMIT_BYTES),
        name="ffn_moe",
    )(tile_expert, tile_rows, src_token, src_token, dst_row, f, gate_row, w1, w3, w2)


def _ple_kernel(h1_ref, y0_ref, y1_ref, p_ref, pg_ref, wg_ref, wp_ref, o_ref):
    h2 = h1_ref[...] + y0_ref[...] + y1_ref[...]
    o_ref[...] = _ple(h2, p_ref[...], pg_ref[...], wg_ref[...], wp_ref[...])


def _ple_combine(h1, y, p, pg, wg, wp, *, tm):
    n, d = h1.shape
    row = lambda i: (i, 0)
    fixed = lambda i: (0, 0)
    return pl.pallas_call(
        _ple_kernel,
        grid=(n // tm,),
        in_specs=[
            pl.BlockSpec((tm, d), row),
            pl.BlockSpec((tm, d), row),
            pl.BlockSpec((tm, d), lambda i: (i + n // tm, 0)),
            pl.BlockSpec((tm, p.shape[1]), row),
            pl.BlockSpec((1, d), fixed),
            pl.BlockSpec(wg.shape, fixed),
            pl.BlockSpec(wp.shape, fixed),
        ],
        out_specs=pl.BlockSpec((tm, d), row),
        out_shape=jax.ShapeDtypeStruct((n, d), F32),
        compiler_params=pltpu.CompilerParams(
            dimension_semantics=("arbitrary",), vmem_limit_bytes=VMEM_LIMIT_BYTES),
        name="ple_combine",
    )(h1, y, y, p, pg, wg, wp)


def kernel(x, p, mix_norm_g, w_in, q_norm_g, k_norm_g, conv_w, conv_b, attn_out_g, conv_out_g,
           w_o, ffn_norm_g, dense_w1, dense_w3, dense_w2, router_w, moe_w1, moe_w3, moe_w2,
           ple_norm_g, ple_gate_w, ple_proj_w):
    bsz, seq_len, d = x.shape
    depth = w_in.shape[0]
    sb = attn_out_g.shape[1]
    n_heads = sb // HEAD_DIM
    n_experts = router_w.shape[2]
    n = bsz * seq_len
    tm = min(ROW_TILE, seq_len)
    t = min(ATTN_TILE, seq_len)
    tm_ffn = min(FFN_ROW_TILE, n)

    head_id = jnp.arange(sb) // HEAD_DIM
    head_mean = jnp.where(head_id[:, None] == head_id[None, :], 1.0 / HEAD_DIM, 0.0).astype(BF16)
    ar = jnp.arange(t)
    tri = (ar[:, None] >= ar[None, :]).astype(BF16)
    q_scale = LOG2E / float(np.sqrt(HEAD_DIM))
    row2 = lambda a: a.reshape(1, -1)

    h = x.reshape(n, d)
    for i in range(depth):
        qg = row2(jnp.tile(q_norm_g[i], n_heads) * q_scale)
        kg = row2(jnp.tile(k_norm_g[i], n_heads))
        q, k, v, cu, b = _mix_in(h, row2(mix_norm_g[i]), w_in[i].astype(BF16), qg, kg,
                                 head_mean, tm=tm)
        attn = _attention(q, k, v, tri, bsz=bsz, t=t, pairs=ATTN_PAIRS)
        j = i // 2
        routed = i % 2 == 1
        rw = None
        if routed:
            e_pad = -(-n_experts // SUBLANES) * SUBLANES
            rw = jnp.pad(router_w[j].T, ((0, e_pad - n_experts), (0, 0)))
            rw_hi = rw.astype(BF16)
            rw = jnp.stack([rw_hi, (rw - rw_hi.astype(F32)).astype(BF16)])
        outs = _mix_out(attn, cu, b, h, row2(attn_out_g[i]), row2(conv_out_g[i]), conv_w[i],
                        row2(conv_b[i]), w_o[i].astype(BF16), row2(ffn_norm_g[i]), rw,
                        tm=tm, seq_len=seq_len, n_experts=n_experts if routed else 0)
        pi = p[i].reshape(n, -1)
        ple_w = (row2(ple_norm_g[i]), ple_gate_w[i].astype(BF16), ple_proj_w[i].astype(BF16))
        if routed:
            h1, f, route = outs
            plan = _route_plan(route, n_experts, MOE_ROW_TILE)
            y = _moe_ffn(f, plan, moe_w1[j].astype(BF16), moe_w3[j].astype(BF16),
                         moe_w2[j].astype(BF16), tm=MOE_ROW_TILE, tf=MOE_COL_TILE)
            h = _ple_combine(h1, y, pi, *ple_w, tm=tm)
        else:
            h1, f = outs
            h = _ffn(f, dense_w1[j:j + 1].astype(BF16), dense_w3[j:j + 1].astype(BF16),
                     dense_w2[j:j + 1].astype(BF16), h1, pi, *ple_w, tm=tm_ffn,
                     tf=FFN_COL_TILE)
    return h.reshape(bsz, seq_len, d)
```

```python
import functools

import numpy as np
import jax
import jax.numpy as jnp
from jax import lax
from jax.experimental import pallas as pl
from jax.experimental.pallas import tpu as pltpu

F32 = jnp.float32
BF16 = jnp.bfloat16

EPS = 1e-6
HEAD_DIM = 64
CONV_K = 3
TOP_K = 2
LANES = 128
SUBLANES = 8
HALO_ROWS = SUBLANES
LOG2E = float(np.log2(np.e))

VMEM_LIMIT_BYTES = 56 * 1024 * 1024

ROW_TILE = 512
ATTN_TILE = 256
ATTN_PAIRS = 2
SP_CLAMP = 64.0
DONE_LOG2 = 160.0
NO_BLOCK_DECAY = 1e30
FFN_ROW_TILE = 1024
FFN_COL_TILE = 512
MOE_ROW_TILE = 512
MOE_COL_TILE = 896


def _rms(x, g):
    return x * lax.rsqrt(jnp.mean(x * x, axis=-1, keepdims=True) + EPS) * g


def _dot(a, b):
    return jnp.dot(a, b, preferred_element_type=F32)


def _mix_in_kernel(h_ref, g_ref, w_ref, qg_ref, kg_ref, hm_ref,
                   q_ref, k_ref, v_ref, cu_ref, b_ref, *, sb):
    a = _rms(h_ref[...], g_ref[...]).astype(BF16)
    hm = hm_ref[...]

    def qk_norm(t, g):
        ms = _dot((t * t).astype(BF16), hm)
        return (t * lax.rsqrt(ms + EPS) * g).astype(BF16)

    q_ref[0] = qk_norm(_dot(a, w_ref[:, 0:sb]), qg_ref[...])
    k_ref[0] = qk_norm(_dot(a, w_ref[:, sb:2 * sb]), kg_ref[...])
    v_ref[0] = _dot(a, w_ref[:, 2 * sb:3 * sb]).astype(BF16)
    u = _dot(a, w_ref[:, 3 * sb:4 * sb])
    c = _dot(a, w_ref[:, 4 * sb:5 * sb])
    cu_ref[...] = c * u
    b_ref[...] = _dot(a, w_ref[:, 5 * sb:6 * sb])


def _mix_in(h, g, w_in, qg, kg, head_mean, *, tm, seq_len):
    n, d = h.shape
    sb = head_mean.shape[0]
    tiles_per_seq = seq_len // tm
    row = lambda i: (i, 0)
    fixed = lambda i: (0, 0)
    seq_row = lambda i: (i // tiles_per_seq, i % tiles_per_seq, 0)
    out_bf = jax.ShapeDtypeStruct((n // seq_len, seq_len, sb), BF16)
    out_f32 = jax.ShapeDtypeStruct((n, sb), F32)
    return pl.pallas_call(
        functools.partial(_mix_in_kernel, sb=sb),
        grid=(n // tm,),
        in_specs=[
            pl.BlockSpec((tm, d), row),
            pl.BlockSpec((1, d), fixed),
            pl.BlockSpec(w_in.shape, fixed),
            pl.BlockSpec((1, sb), fixed),
            pl.BlockSpec((1, sb), fixed),
            pl.BlockSpec((sb, sb), fixed),
        ],
        out_specs=[pl.BlockSpec((1, tm, sb), seq_row)] * 3 + [pl.BlockSpec((tm, sb), row)] * 2,
        out_shape=[out_bf, out_bf, out_bf, out_f32, out_f32],
        compiler_params=pltpu.CompilerParams(
            dimension_semantics=("arbitrary",), vmem_limit_bytes=VMEM_LIMIT_BYTES),
        name="mix_in",
    )(h, g, w_in, qg, kg, head_mean)


def _attn_kernel(q_ref, k_ref, v_ref, u_ref, o_ref, acc_ref, c_ref, *, t, pairs):
    i = pl.program_id(2)
    heads = 2 * pairs
    q = q_ref[0]
    lane = lax.broadcasted_iota(jnp.int32, (t, LANES), 1)
    lo_half = lane < HEAD_DIM
    q_heads = []
    for p in range(pairs):
        qp = q[:, p * LANES:(p + 1) * LANES]
        q_heads += [jnp.where(lo_half, qp, jnp.zeros_like(qp)),
                    jnp.where(lo_half, jnp.zeros_like(qp), qp)]
    tri = u_ref[...]
    row = lax.broadcasted_iota(jnp.int32, (t, t), 0)
    col = lax.broadcasted_iota(jnp.int32, (t, t), 1)
    causal = col < row

    acc_ref[...] = jnp.zeros_like(acc_ref)
    c_ref[...] = jnp.zeros_like(c_ref)

    def block(j, diagonal, extra_decay=0.0):
        start = pl.multiple_of(j * t, t)
        kj = k_ref[0, pl.ds(start, t), :]
        vj = v_ref[0, pl.ds(start, t), :]
        for hd in range(heads):
            p = hd // 2
            z = lax.dot_general(q_heads[hd], kj[:, p * LANES:(p + 1) * LANES],
                                (((1,), (1,)), ((), ())), preferred_element_type=F32)
            sp = jnp.maximum(z, jnp.log2(1.0 + jnp.exp2(jnp.minimum(z, SP_CLAMP))))
            if diagonal:
                sp = jnp.where(causal, sp, 0.0)
            incl = _dot(sp.astype(BF16), tri)
            w = jnp.exp2(z - incl - (c_ref[hd] + extra_decay))
            if diagonal:
                w = jnp.where(causal, w, 0.0)
            acc_ref[hd] += _dot(w.astype(BF16), vj[:, p * LANES:(p + 1) * LANES])
            c_ref[hd] += incl[:, 0:1]

    def decay_floor():
        m = jnp.min(c_ref[0])
        for hd in range(1, heads):
            m = jnp.minimum(m, jnp.min(c_ref[hd]))
        return m

    block(i, True)
    block(jnp.maximum(i - 1, 0), False, extra_decay=jnp.where(i > 0, 0.0, NO_BLOCK_DECAY))

    def cond(carry):
        j, floor = carry
        return (j >= 0) & (floor < DONE_LOG2)

    def body(carry):
        j, _ = carry
        block(j, False)
        return j - 1, decay_floor()

    lax.while_loop(cond, body, (i - 2, decay_floor()))
    for p in range(pairs):
        o_ref[0, :, p * LANES:(p + 1) * LANES] = jnp.where(
            lo_half, acc_ref[2 * p], acc_ref[2 * p + 1]).astype(o_ref.dtype)


def _attention(q, k, v, tri, *, t, pairs):
    bsz, s, sb = q.shape
    width = LANES * pairs
    blk = lambda b, g, i: (b, i, g)
    seq = lambda b, g, i: (b, 0, g)
    return pl.pallas_call(
        functools.partial(_attn_kernel, t=t, pairs=pairs),
        grid=(bsz, sb // width, s // t),
        in_specs=[
            pl.BlockSpec((1, t, width), blk),
            pl.BlockSpec((1, s, width), seq),
            pl.BlockSpec((1, s, width), seq),
            pl.BlockSpec((t, t), lambda b, g, i: (0, 0)),
        ],
        out_specs=pl.BlockSpec((1, t, width), blk),
        out_shape=jax.ShapeDtypeStruct((bsz, s, sb), BF16),
        scratch_shapes=[pltpu.VMEM((2 * pairs, t, LANES), F32),
                        pltpu.VMEM((2 * pairs, t, 1), F32)],
        compiler_params=pltpu.CompilerParams(
            dimension_semantics=("arbitrary", "arbitrary", "arbitrary"),
            vmem_limit_bytes=VMEM_LIMIT_BYTES),
        name="attention",
    )(q, k, v, tri)


def _mix_out_kernel(*refs, tiles_per_seq, sb, n_experts):
    (attn_ref, cu_ref, halo_ref, b_ref, h_ref, ag_ref, cg_ref, cw_ref, cb_ref,
     wo_ref, fg_ref) = refs[:11]
    if n_experts:
        rw_ref, h1_ref, f_ref, route_ref = refs[11:]
    else:
        h1_ref, f_ref = refs[11:]

    attn_n = _rms(attn_ref[0].astype(F32), ag_ref[...])

    cu = cu_ref[...]
    first = pl.program_id(0) % tiles_per_seq == 0
    halo = jnp.where(first, 0.0, halo_ref[...])
    rowi = lax.broadcasted_iota(jnp.int32, cu.shape, 0)
    prev1 = jnp.where(rowi == 0, halo[HALO_ROWS - 1:HALO_ROWS], pltpu.roll(cu, 1, axis=0))
    prev2 = jnp.where(rowi == 0, halo[HALO_ROWS - 2:HALO_ROWS - 1],
                      jnp.where(rowi == 1, halo[HALO_ROWS - 1:HALO_ROWS],
                                pltpu.roll(cu, 2, axis=0)))
    conv = cw_ref[0:1] * prev2 + cw_ref[1:2] * prev1 + cw_ref[2:3] * cu + cb_ref[...]
    conv_n = _rms(b_ref[...] * conv, cg_ref[...])

    h1 = (h_ref[...] + _dot(attn_n.astype(BF16), wo_ref[0:sb])
          + _dot(conv_n.astype(BF16), wo_ref[sb:]))
    h1_ref[...] = h1
    f = _rms(h1, fg_ref[...])
    f_ref[...] = f.astype(f_ref.dtype)

    if n_experts:
        f_hi = f.astype(BF16)
        f_lo = (f - f_hi.astype(F32)).astype(BF16)
        nt = (((1,), (1,)), ((), ()))
        logits = (lax.dot_general(rw_ref[0], f_hi, nt, preferred_element_type=F32)
                  + (lax.dot_general(rw_ref[0], f_lo, nt, preferred_element_type=F32)
                     + lax.dot_general(rw_ref[1], f_hi, nt, preferred_element_type=F32)))
        e_pad = logits.shape[0]
        eid = lax.broadcasted_iota(jnp.int32, logits.shape, 0).astype(F32)
        neg = jnp.float32(-jnp.inf)
        logits = jnp.where(eid < n_experts, logits, neg)
        m1 = jnp.max(logits, axis=0, keepdims=True)
        i1 = jnp.min(jnp.where(logits == m1, eid, float(e_pad)), axis=0, keepdims=True)
        rest = jnp.where(eid == i1, neg, logits)
        m2 = jnp.max(rest, axis=0, keepdims=True)
        i2 = jnp.min(jnp.where(rest == m2, eid, float(e_pad)), axis=0, keepdims=True)
        e2 = jnp.exp(m2 - m1)
        g1 = 1.0 / (1.0 + e2)
        g2 = e2 * g1
        slot = lax.broadcasted_iota(jnp.int32, route_ref.shape, 0)
        route_ref[...] = jnp.where(
            slot == 0, i1, jnp.where(slot == 1, i2, jnp.where(
                slot == 2, g1, jnp.where(slot == 3, g2, 0.0))))


def _mix_out(attn, cu, b, h, ag, cg, cw, cb, w_o, fg, router_w, *, tm, seq_len, n_experts):
    n, d = h.shape
    sb = attn.shape[2]
    tiles_per_seq = seq_len // tm
    row = lambda i: (i, 0)
    fixed = lambda i: (0, 0)
    halo = lambda i: (jnp.maximum(i * (tm // HALO_ROWS) - 1, 0), 0)
    in_specs = [
        pl.BlockSpec((1, tm, sb), lambda i: (i // tiles_per_seq, i % tiles_per_seq, 0)),
        pl.BlockSpec((tm, sb), row),
        pl.BlockSpec((HALO_ROWS, sb), halo),
        pl.BlockSpec((tm, sb), row),
        pl.BlockSpec((tm, d), row),
        pl.BlockSpec((1, sb), fixed),
        pl.BlockSpec((1, sb), fixed),
        pl.BlockSpec((CONV_K, sb), fixed),
        pl.BlockSpec((1, sb), fixed),
        pl.BlockSpec(w_o.shape, fixed),
        pl.BlockSpec((1, d), fixed),
    ]
    args = [attn, cu, cu, b, h, ag, cg, cw, cb, w_o, fg]
    out_specs = [pl.BlockSpec((tm, d), row), pl.BlockSpec((tm, d), row)]
    out_shape = [jax.ShapeDtypeStruct((n, d), F32),
                 jax.ShapeDtypeStruct((n, d), F32 if n_experts else BF16)]
    if n_experts:
        in_specs.append(pl.BlockSpec(router_w.shape, lambda i: (0, 0, 0)))
        args.append(router_w)
        out_specs.append(pl.BlockSpec((SUBLANES, tm), lambda i: (0, i)))
        out_shape.append(jax.ShapeDtypeStruct((SUBLANES, n), F32))
    return pl.pallas_call(
        functools.partial(_mix_out_kernel, tiles_per_seq=seq_len // tm, sb=sb,
                          n_experts=n_experts),
        grid=(n // tm,),
        in_specs=in_specs,
        out_specs=out_specs,
        out_shape=out_shape,
        compiler_params=pltpu.CompilerParams(
            dimension_semantics=("arbitrary",), vmem_limit_bytes=VMEM_LIMIT_BYTES),
        name="mix_out_moe" if n_experts else "mix_out",
    )(*args)


def _swiglu_step(x, w1_ref, w3_ref, w2_ref):
    a = _dot(x, w1_ref[0])
    hidden = a * jax.nn.sigmoid(a) * _dot(x, w3_ref[0])
    return _dot(hidden.astype(BF16), w2_ref[0])


def _ple(h2, p, pg, wg, wp):
    gate = jax.nn.sigmoid(_dot(_rms(h2, pg).astype(BF16), wg))
    return h2 + gate * _dot(p.astype(BF16), wp)


def _ffn_kernel(f_ref, w1_ref, w3_ref, w2_ref, h1_ref, p_ref, pg_ref, wg_ref, wp_ref,
                o_ref, acc_ref):
    c = pl.program_id(1)

    @pl.when(c == 0)
    def _():
        acc_ref[...] = jnp.zeros_like(acc_ref)

    acc_ref[...] += _swiglu_step(f_ref[...], w1_ref, w3_ref, w2_ref)

    @pl.when(c == pl.num_programs(1) - 1)
    def _():
        o_ref[...] = _ple(h1_ref[...] + acc_ref[...], p_ref[...], pg_ref[...], wg_ref[...],
                          wp_ref[...])


def _ffn(f, w1, w3, w2, h1, p, pg, wg, wp, *, tm, tf):
    n, d = h1.shape
    d_ff = w1.shape[2]
    row = lambda i, c: (i, 0)
    fixed = lambda i, c: (0, 0)
    in_specs = [
        pl.BlockSpec((tm, d), row),
        pl.BlockSpec((1, d, tf), lambda i, c: (0, 0, c)),
        pl.BlockSpec((1, d, tf), lambda i, c: (0, 0, c)),
        pl.BlockSpec((1, tf, d), lambda i, c: (0, c, 0)),
        pl.BlockSpec((tm, d), row),
        pl.BlockSpec((tm, p.shape[1]), row),
        pl.BlockSpec((1, d), fixed),
        pl.BlockSpec(wg.shape, fixed),
        pl.BlockSpec(wp.shape, fixed),
    ]
    return pl.pallas_call(
        _ffn_kernel,
        grid=(n // tm, d_ff // tf),
        in_specs=in_specs,
        out_specs=pl.BlockSpec((tm, d), row),
        out_shape=jax.ShapeDtypeStruct((n, d), F32),
        scratch_shapes=[pltpu.VMEM((tm, d), F32)],
        compiler_params=pltpu.CompilerParams(
            dimension_semantics=("arbitrary", "arbitrary"),
            vmem_limit_bytes=VMEM_LIMIT_BYTES),
        name="ffn_dense",
    )(f, w1, w3, w2, h1, p, pg, wg, wp)


def _route_plan(route, n_experts, tm):
    n = route.shape[1]
    pairs = TOP_K * n
    n_rows = pairs + n_experts * tm
    eid = route[0:TOP_K].T.astype(jnp.int32).reshape(pairs)
    gates = route[TOP_K:2 * TOP_K].T.reshape(pairs)
    pair_bits = int(np.ceil(np.log2(pairs)))
    keys = jnp.sort(eid * (1 << pair_bits) + jnp.arange(pairs, dtype=jnp.int32))
    order = keys & ((1 << pair_bits) - 1)
    bounds = jnp.arange(n_experts + 1, dtype=jnp.int32) << pair_bits
    bounds = bounds.at[n_experts].set(jnp.iinfo(jnp.int32).max)
    first = jnp.sum(keys[None, :] < bounds[:, None], axis=1).astype(jnp.int32)
    counts = first[1:] - first[:-1]
    padded = (counts + tm - 1) // tm * tm
    ends = jnp.cumsum(padded)
    n_tiles = n_rows // tm
    tile_start = jnp.arange(n_tiles, dtype=jnp.int32) * tm
    tile_expert = jnp.sum(tile_start[:, None] >= ends[None, :], axis=1)
    tile_expert = jnp.minimum(tile_expert, n_experts - 1).astype(jnp.int32)
    in_group = tile_start - (ends - padded)[tile_expert]
    tile_rows = jnp.clip(counts[tile_expert] - in_group, 0, tm)
    lane = jnp.arange(tm, dtype=jnp.int32)[None, :]
    real = lane < tile_rows[:, None]
    safe = order[jnp.clip((first[tile_expert] + in_group)[:, None] + lane, 0, pairs - 1)]
    src_token = jnp.where(real, safe // TOP_K, 0)
    gate_row = jnp.where(real, gates[safe], 0.0)
    spare = pairs + (jnp.arange(n_tiles, dtype=jnp.int32) % 2 * tm)[:, None] + lane
    dst_row = jnp.where(real, (safe % TOP_K) * n + safe // TOP_K, spare)
    src_token = jnp.concatenate([src_token, jnp.zeros((1, tm), jnp.int32)])
    dst_row = jnp.concatenate([pairs + tm + lane, dst_row])
    return (src_token.reshape(n_tiles + 1, 1, tm), dst_row.reshape(n_tiles + 1, 1, tm),
            gate_row.reshape(n_rows, 1), tile_expert, tile_rows)


def _for_each_row(count, fn, unroll=8):
    if isinstance(count, int):
        full = count // unroll
    else:
        full = lax.shift_right_logical(count, jnp.int32(int(np.log2(unroll))))

    def chunk(i, carry):
        for u in range(unroll):
            fn(i * unroll + u)
        return carry

    def single(r, carry):
        fn(r)
        return carry

    lax.fori_loop(0, full, chunk, 0)
    lax.fori_loop(full * unroll, count, single, 0)


def _moe_ffn_kernel(expert_ref, count_ref, src0_ref, src_ref, dst_ref, f_hbm, gate_ref,
                    w1_ref, w3_ref, w2_ref, y_hbm, rows_ref, x_ref, acc_ref, stage_ref,
                    gather_sem, scatter_sem, *, tm, n_chunks, n_pairs):
    j = pl.program_id(0)
    c = pl.program_id(1)
    last_tile = pl.num_programs(0) - 1
    last_chunk = n_chunks - 1
    per_step = tm // n_chunks
    active = count_ref[j] > 0
    cur = j % 2
    other = 1 - cur

    def tile_gather(slot):
        return pltpu.make_async_copy(f_hbm.at[pl.ds(0, tm), :], rows_ref.at[slot],
                                     gather_sem.at[slot])

    def tile_scatter(slot):
        return pltpu.make_async_copy(stage_ref.at[slot], y_hbm.at[pl.ds(0, tm), :],
                                     scatter_sem.at[slot])

    @pl.when((j == 0) & (c == 0))
    def _():
        stage_ref[...] = jnp.zeros_like(stage_ref)
        for slot in range(2):
            spare = pltpu.make_async_copy(stage_ref.at[slot],
                                          y_hbm.at[pl.ds(n_pairs + slot * tm, tm), :],
                                          scatter_sem.at[slot])
            spare.start()
            spare.wait()

        def gather_first(r):
            pltpu.make_async_copy(f_hbm.at[pl.ds(src0_ref[0, 0, r], 1), :],
                                  rows_ref.at[0, pl.ds(r, 1), :], gather_sem.at[0]).start()
        _for_each_row(tm, gather_first)

    @pl.when(c == 0)
    def _():
        tile_gather(cur).wait()

        @pl.when(j >= 1)
        def _():
            tile_scatter(cur).wait()

        x_ref[...] = rows_ref[cur].astype(BF16)
        acc_ref[...] = jnp.zeros_like(acc_ref)

    def start_row_copies():
        base = pl.multiple_of(c * per_step, per_step)
        for r in range(per_step):
            pltpu.make_async_copy(f_hbm.at[pl.ds(src_ref[0, 0, base + r], 1), :],
                                  rows_ref.at[other, pl.ds(base + r, 1), :],
                                  gather_sem.at[other]).start()
            pltpu.make_async_copy(stage_ref.at[other, pl.ds(base + r, 1), :],
                                  y_hbm.at[pl.ds(dst_ref[0, 0, base + r], 1), :],
                                  scatter_sem.at[other]).start()

    @pl.when(active)
    def _():
        start_row_copies()
        acc_ref[...] += _swiglu_step(x_ref[...], w1_ref, w3_ref, w2_ref)

    @pl.when(jnp.logical_not(active))
    def _():
        start_row_copies()

    @pl.when(active & (c == last_chunk))
    def _():
        stage_ref[cur] = acc_ref[...] * gate_ref[...]

    @pl.when((j == last_tile) & (c == last_chunk))
    def _():
        tile_gather(other).wait()
        tile_scatter(other).wait()


def _moe_ffn(f, plan, w1, w3, w2, *, tm, tf):
    src_token, dst_row, gate_row, tile_expert, tile_rows = plan
    n, d = f.shape
    n_exp, _, d_ff = w1.shape
    n_tiles = tile_rows.shape[0]
    n_chunks = d_ff // tf
    n_pairs = TOP_K * n
    assert n_pairs % tm == 0 and tm % n_chunks == 0 and n_exp < tm
    chunk = lambda j, c, count: jnp.where(count[j] > 0, c, n_chunks - 1)
    grid_spec = pltpu.PrefetchScalarGridSpec(
        num_scalar_prefetch=2,
        grid=(n_tiles, n_chunks),
        in_specs=[
            pl.BlockSpec((1, 1, tm), lambda j, c, ex, count: (0, 0, 0), memory_space=pltpu.SMEM),
            pl.BlockSpec((1, 1, tm), lambda j, c, ex, count: (j + 1, 0, 0),
                         memory_space=pltpu.SMEM),
            pl.BlockSpec((1, 1, tm), lambda j, c, ex, count: (j, 0, 0), memory_space=pltpu.SMEM),
            pl.BlockSpec(memory_space=pl.ANY),
            pl.BlockSpec((tm, 1), lambda j, c, ex, count: (j, 0)),
            pl.BlockSpec((1, d, tf), lambda j, c, ex, count: (ex[j], 0, chunk(j, c, count))),
            pl.BlockSpec((1, d, tf), lambda j, c, ex, count: (ex[j], 0, chunk(j, c, count))),
            pl.BlockSpec((1, tf, d), lambda j, c, ex, count: (ex[j], chunk(j, c, count), 0)),
        ],
        out_specs=pl.BlockSpec(memory_space=pl.ANY),
        scratch_shapes=[pltpu.VMEM((2, tm, d), F32), pltpu.VMEM((tm, d), BF16),
                        pltpu.VMEM((tm, d), F32), pltpu.VMEM((2, tm, d), F32),
                        pltpu.SemaphoreType.DMA((2,)), pltpu.SemaphoreType.DMA((2,))],
    )
    return pl.pallas_call(
        functools.partial(_moe_ffn_kernel, tm=tm, n_chunks=n_chunks, n_pairs=n_pairs),
        grid_spec=grid_spec,
        out_shape=jax.ShapeDtypeStruct((n_pairs + 2 * tm, d), F32),
        compiler_params=pltpu.CompilerParams(
            dimension_semantics=("arbitrary", "arbitrary"),
            vmem_limit_bytes=VMEM_LIMIT_BYTES),
        name="ffn_moe",
    )(tile_expert, tile_rows, src_token, src_token, dst_row, f, gate_row, w1, w3, w2)


def _ple_kernel(h1_ref, y0_ref, y1_ref, p_ref, pg_ref, wg_ref, wp_ref, o_ref):
    h2 = h1_ref[...] + y0_ref[...] + y1_ref[...]
    o_ref[...] = _ple(h2, p_ref[...], pg_ref[...], wg_ref[...], wp_ref[...])


def _ple_combine(h1, y, p, pg, wg, wp, *, tm):
    n, d = h1.shape
    row = lambda i: (i, 0)
    fixed = lambda i: (0, 0)
    return pl.pallas_call(
        _ple_kernel,
        grid=(n // tm,),
        in_specs=[
            pl.BlockSpec((tm, d), row),
            pl.BlockSpec((tm, d), row),
            pl.BlockSpec((tm, d), lambda i: (i + n // tm, 0)),
            pl.BlockSpec((tm, p.shape[1]), row),
            pl.BlockSpec((1, d), fixed),
            pl.BlockSpec(wg.shape, fixed),
            pl.BlockSpec(wp.shape, fixed),
        ],
        out_specs=pl.BlockSpec((tm, d), row),
        out_shape=jax.ShapeDtypeStruct((n, d), F32),
        compiler_params=pltpu.CompilerParams(
            dimension_semantics=("arbitrary",), vmem_limit_bytes=VMEM_LIMIT_BYTES),
        name="ple_combine",
    )(h1, y, y, p, pg, wg, wp)


def kernel(x, p, mix_norm_g, w_in, q_norm_g, k_norm_g, conv_w, conv_b, attn_out_g, conv_out_g,
           w_o, ffn_norm_g, dense_w1, dense_w3, dense_w2, router_w, moe_w1, moe_w3, moe_w2,
           ple_norm_g, ple_gate_w, ple_proj_w):
    bsz, seq_len, d = x.shape
    depth = w_in.shape[0]
    sb = attn_out_g.shape[1]
    n_heads = sb // HEAD_DIM
    n_experts = router_w.shape[2]
    n = bsz * seq_len
    tm = min(ROW_TILE, seq_len)
    t = min(ATTN_TILE, seq_len)
    tm_ffn = min(FFN_ROW_TILE, n)

    head_id = jnp.arange(sb) // HEAD_DIM
    head_mean = jnp.where(head_id[:, None] == head_id[None, :], 1.0 / HEAD_DIM, 0.0).astype(BF16)
    ar = jnp.arange(t)
    tri = (ar[:, None] >= ar[None, :]).astype(BF16)
    q_scale = LOG2E / float(np.sqrt(HEAD_DIM))
    row2 = lambda a: a.reshape(1, -1)

    h = x.reshape(n, d)
    for i in range(depth):
        qg = row2(jnp.tile(q_norm_g[i], n_heads) * q_scale)
        kg = row2(jnp.tile(k_norm_g[i], n_heads))
        q, k, v, cu, b = _mix_in(h, row2(mix_norm_g[i]), w_in[i].astype(BF16), qg, kg,
                                 head_mean, tm=tm, seq_len=seq_len)
        attn = _attention(q, k, v, tri, t=t, pairs=ATTN_PAIRS)
        j = i // 2
        routed = i % 2 == 1
        rw = None
        if routed:
            e_pad = -(-n_experts // SUBLANES) * SUBLANES
            rw = jnp.pad(router_w[j].T, ((0, e_pad - n_experts), (0, 0)))
            rw_hi = rw.astype(BF16)
            rw = jnp.stack([rw_hi, (rw - rw_hi.astype(F32)).astype(BF16)])
        outs = _mix_out(attn, cu, b, h, row2(attn_out_g[i]), row2(conv_out_g[i]), conv_w[i],
                        row2(conv_b[i]), w_o[i].astype(BF16), row2(ffn_norm_g[i]), rw,
                        tm=tm, seq_len=seq_len, n_experts=n_experts if routed else 0)
        pi = p[i].reshape(n, -1)
        ple_w = (row2(ple_norm_g[i]), ple_gate_w[i].astype(BF16), ple_proj_w[i].astype(BF16))
        if routed:
            h1, f, route = outs
            plan = _route_plan(route, n_experts, MOE_ROW_TILE)
            y = _moe_ffn(f, plan, moe_w1[j].astype(BF16), moe_w3[j].astype(BF16),
                         moe_w2[j].astype(BF16), tm=MOE_ROW_TILE, tf=MOE_COL_TILE)
            h = _ple_combine(h1, y, pi, *ple_w, tm=tm)
        else:
            h1, f = outs
            h = _ffn(f, dense_w1[j:j + 1].astype(BF16), dense_w3[j:j + 1].astype(BF16),
                     dense_w2[j:j + 1].astype(BF16), h1, pi, *ple_w, tm=tm_ffn,
                     tf=FFN_COL_TILE)
    return h.reshape(bsz, seq_len, d)
```

```python
import functools

import numpy as np
import jax
import jax.numpy as jnp
from jax import lax
from jax.experimental import pallas as pl
from jax.experimental.pallas import tpu as pltpu

F32 = jnp.float32
BF16 = jnp.bfloat16

EPS = 1e-6
HEAD_DIM = 64
CONV_K = 3
TOP_K = 2
LANES = 128
SUBLANES = 8
HALO_ROWS = SUBLANES
LOG2E = float(np.log2(np.e))

VMEM_LIMIT_BYTES = 56 * 1024 * 1024

ROW_TILE = 512
ATTN_TILE = 256
ATTN_PAIRS = 2
SP_CLAMP = 64.0
DONE_LOG2 = 160.0
NO_BLOCK_DECAY = 1e30
FFN_ROW_TILE = 1024
FFN_COL_TILE = 512
MOE_ROW_TILE = 512
MOE_COL_TILE = 1792


def _rms(x, g):
    return x * lax.rsqrt(jnp.mean(x * x, axis=-1, keepdims=True) + EPS) * g


def _dot(a, b):
    return jnp.dot(a, b, preferred_element_type=F32)


def _mix_in_kernel(h_ref, g_ref, w_ref, qg_ref, kg_ref, hm_ref,
                   q_ref, k_ref, v_ref, cu_ref, b_ref, *, sb):
    a = _rms(h_ref[...], g_ref[...]).astype(BF16)
    hm = hm_ref[...]

    def qk_norm(t, g):
        ms = _dot((t * t).astype(BF16), hm)
        return (t * lax.rsqrt(ms + EPS) * g).astype(BF16)

    q_ref[0] = qk_norm(_dot(a, w_ref[:, 0:sb]), qg_ref[...])
    k_ref[0] = qk_norm(_dot(a, w_ref[:, sb:2 * sb]), kg_ref[...])
    v_ref[0] = _dot(a, w_ref[:, 2 * sb:3 * sb]).astype(BF16)
    u = _dot(a, w_ref[:, 3 * sb:4 * sb])
    c = _dot(a, w_ref[:, 4 * sb:5 * sb])
    cu_ref[...] = c * u
    b_ref[...] = _dot(a, w_ref[:, 5 * sb:6 * sb])


def _mix_in(h, g, w_in, qg, kg, head_mean, *, tm, seq_len):
    n, d = h.shape
    sb = head_mean.shape[0]
    tiles_per_seq = seq_len // tm
    row = lambda i: (i, 0)
    fixed = lambda i: (0, 0)
    seq_row = lambda i: (i // tiles_per_seq, i % tiles_per_seq, 0)
    out_bf = jax.ShapeDtypeStruct((n // seq_len, seq_len, sb), BF16)
    out_f32 = jax.ShapeDtypeStruct((n, sb), F32)
    return pl.pallas_call(
        functools.partial(_mix_in_kernel, sb=sb),
        grid=(n // tm,),
        in_specs=[
            pl.BlockSpec((tm, d), row),
            pl.BlockSpec((1, d), fixed),
            pl.BlockSpec(w_in.shape, fixed),
            pl.BlockSpec((1, sb), fixed),
            pl.BlockSpec((1, sb), fixed),
            pl.BlockSpec((sb, sb), fixed),
        ],
        out_specs=[pl.BlockSpec((1, tm, sb), seq_row)] * 3 + [pl.BlockSpec((tm, sb), row)] * 2,
        out_shape=[out_bf, out_bf, out_bf, out_f32, out_f32],
        compiler_params=pltpu.CompilerParams(
            dimension_semantics=("arbitrary",), vmem_limit_bytes=VMEM_LIMIT_BYTES),
        name="mix_in",
    )(h, g, w_in, qg, kg, head_mean)


def _attn_kernel(q_ref, k_ref, v_ref, u_ref, o_ref, acc_ref, c_ref, *, t, pairs):
    i = pl.program_id(2)
    heads = 2 * pairs
    q = q_ref[0]
    lane = lax.broadcasted_iota(jnp.int32, (t, LANES), 1)
    lo_half = lane < HEAD_DIM
    q_heads = []
    for p in range(pairs):
        qp = q[:, p * LANES:(p + 1) * LANES]
        q_heads += [jnp.where(lo_half, qp, jnp.zeros_like(qp)),
                    jnp.where(lo_half, jnp.zeros_like(qp), qp)]
    tri = u_ref[...]
    row = lax.broadcasted_iota(jnp.int32, (t, t), 0)
    col = lax.broadcasted_iota(jnp.int32, (t, t), 1)
    causal = col < row

    acc_ref[...] = jnp.zeros_like(acc_ref)
    c_ref[...] = jnp.zeros_like(c_ref)

    def block(j, diagonal, extra_decay=0.0):
        start = pl.multiple_of(j * t, t)
        kj = k_ref[0, pl.ds(start, t), :]
        vj = v_ref[0, pl.ds(start, t), :]
        for hd in range(heads):
            p = hd // 2
            z = lax.dot_general(q_heads[hd], kj[:, p * LANES:(p + 1) * LANES],
                                (((1,), (1,)), ((), ())), preferred_element_type=F32)
            sp = jnp.maximum(z, jnp.log2(1.0 + jnp.exp2(jnp.minimum(z, SP_CLAMP))))
            if diagonal:
                sp = jnp.where(causal, sp, 0.0)
            incl = _dot(sp.astype(BF16), tri)
            w = jnp.exp2(z - incl - (c_ref[hd] + extra_decay))
            if diagonal:
                w = jnp.where(causal, w, 0.0)
            acc_ref[hd] += _dot(w.astype(BF16), vj[:, p * LANES:(p + 1) * LANES])
            c_ref[hd] += incl[:, 0:1]

    def decay_floor():
        m = jnp.min(c_ref[0])
        for hd in range(1, heads):
            m = jnp.minimum(m, jnp.min(c_ref[hd]))
        return m

    block(i, True)
    block(jnp.maximum(i - 1, 0), False, extra_decay=jnp.where(i > 0, 0.0, NO_BLOCK_DECAY))

    def cond(carry):
        j, floor = carry
        return (j >= 0) & (floor < DONE_LOG2)

    def body(carry):
        j, _ = carry
        block(j, False)
        return j - 1, decay_floor()

    lax.while_loop(cond, body, (i - 2, decay_floor()))
    for p in range(pairs):
        o_ref[0, :, p * LANES:(p + 1) * LANES] = jnp.where(
            lo_half, acc_ref[2 * p], acc_ref[2 * p + 1]).astype(o_ref.dtype)


def _attention(q, k, v, tri, *, t, pairs):
    bsz, s, sb = q.shape
    width = LANES * pairs
    blk = lambda b, g, i: (b, i, g)
    seq = lambda b, g, i: (b, 0, g)
    return pl.pallas_call(
        functools.partial(_attn_kernel, t=t, pairs=pairs),
        grid=(bsz, sb // width, s // t),
        in_specs=[
            pl.BlockSpec((1, t, width), blk),
            pl.BlockSpec((1, s, width), seq),
            pl.BlockSpec((1, s, width), seq),
            pl.BlockSpec((t, t), lambda b, g, i: (0, 0)),
        ],
        out_specs=pl.BlockSpec((1, t, width), blk),
        out_shape=jax.ShapeDtypeStruct((bsz, s, sb), BF16),
        scratch_shapes=[pltpu.VMEM((2 * pairs, t, LANES), F32),
                        pltpu.VMEM((2 * pairs, t, 1), F32)],
        compiler_params=pltpu.CompilerParams(
            dimension_semantics=("arbitrary", "arbitrary", "arbitrary"),
            vmem_limit_bytes=VMEM_LIMIT_BYTES),
        name="attention",
    )(q, k, v, tri)


def _mix_out_kernel(*refs, tiles_per_seq, sb, n_experts):
    (attn_ref, cu_ref, halo_ref, b_ref, h_ref, ag_ref, cg_ref, cw_ref, cb_ref,
     wo_ref, fg_ref) = refs[:11]
    if n_experts:
        rw_ref, h1_ref, f_ref, route_ref = refs[11:]
    else:
        h1_ref, f_ref = refs[11:]

    attn_n = _rms(attn_ref[0].astype(F32), ag_ref[...])

    cu = cu_ref[...]
    first = pl.program_id(0) % tiles_per_seq == 0
    halo = jnp.where(first, 0.0, halo_ref[...])
    rowi = lax.broadcasted_iota(jnp.int32, cu.shape, 0)
    prev1 = jnp.where(rowi == 0, halo[HALO_ROWS - 1:HALO_ROWS], pltpu.roll(cu, 1, axis=0))
    prev2 = jnp.where(rowi == 0, halo[HALO_ROWS - 2:HALO_ROWS - 1],
                      jnp.where(rowi == 1, halo[HALO_ROWS - 1:HALO_ROWS],
                                pltpu.roll(cu, 2, axis=0)))
    conv = cw_ref[0:1] * prev2 + cw_ref[1:2] * prev1 + cw_ref[2:3] * cu + cb_ref[...]
    conv_n = _rms(b_ref[...] * conv, cg_ref[...])

    h1 = (h_ref[...] + _dot(attn_n.astype(BF16), wo_ref[0:sb])
          + _dot(conv_n.astype(BF16), wo_ref[sb:]))
    h1_ref[...] = h1
    f = _rms(h1, fg_ref[...])
    f_ref[...] = f.astype(f_ref.dtype)

    if n_experts:
        f_hi = f.astype(BF16)
        f_lo = (f - f_hi.astype(F32)).astype(BF16)
        nt = (((1,), (1,)), ((), ()))
        logits = (lax.dot_general(rw_ref[0], f_hi, nt, preferred_element_type=F32)
                  + (lax.dot_general(rw_ref[0], f_lo, nt, preferred_element_type=F32)
                     + lax.dot_general(rw_ref[1], f_hi, nt, preferred_element_type=F32)))
        e_pad = logits.shape[0]
        eid = lax.broadcasted_iota(jnp.int32, logits.shape, 0).astype(F32)
        neg = jnp.float32(-jnp.inf)
        logits = jnp.where(eid < n_experts, logits, neg)
        m1 = jnp.max(logits, axis=0, keepdims=True)
        i1 = jnp.min(jnp.where(logits == m1, eid, float(e_pad)), axis=0, keepdims=True)
        rest = jnp.where(eid == i1, neg, logits)
        m2 = jnp.max(rest, axis=0, keepdims=True)
        i2 = jnp.min(jnp.where(rest == m2, eid, float(e_pad)), axis=0, keepdims=True)
        e2 = jnp.exp(m2 - m1)
        g1 = 1.0 / (1.0 + e2)
        g2 = e2 * g1
        slot = lax.broadcasted_iota(jnp.int32, route_ref.shape, 0)
        route_ref[...] = jnp.where(
            slot == 0, i1, jnp.where(slot == 1, i2, jnp.where(
                slot == 2, g1, jnp.where(slot == 3, g2, 0.0))))


def _mix_out(attn, cu, b, h, ag, cg, cw, cb, w_o, fg, router_w, *, tm, seq_len, n_experts):
    n, d = h.shape
    sb = attn.shape[2]
    tiles_per_seq = seq_len // tm
    row = lambda i: (i, 0)
    fixed = lambda i: (0, 0)
    halo = lambda i: (jnp.maximum(i * (tm // HALO_ROWS) - 1, 0), 0)
    in_specs = [
        pl.BlockSpec((1, tm, sb), lambda i: (i // tiles_per_seq, i % tiles_per_seq, 0)),
        pl.BlockSpec((tm, sb), row),
        pl.BlockSpec((HALO_ROWS, sb), halo),
        pl.BlockSpec((tm, sb), row),
        pl.BlockSpec((tm, d), row),
        pl.BlockSpec((1, sb), fixed),
        pl.BlockSpec((1, sb), fixed),
        pl.BlockSpec((CONV_K, sb), fixed),
        pl.BlockSpec((1, sb), fixed),
        pl.BlockSpec(w_o.shape, fixed),
        pl.BlockSpec((1, d), fixed),
    ]
    args = [attn, cu, cu, b, h, ag, cg, cw, cb, w_o, fg]
    out_specs = [pl.BlockSpec((tm, d), row), pl.BlockSpec((tm, d), row)]
    out_shape = [jax.ShapeDtypeStruct((n, d), F32),
                 jax.ShapeDtypeStruct((n, d), F32 if n_experts else BF16)]
    if n_experts:
        in_specs.append(pl.BlockSpec(router_w.shape, lambda i: (0, 0, 0)))
        args.append(router_w)
        out_specs.append(pl.BlockSpec((SUBLANES, tm), lambda i: (0, i)))
        out_shape.append(jax.ShapeDtypeStruct((SUBLANES, n), F32))
    return pl.pallas_call(
        functools.partial(_mix_out_kernel, tiles_per_seq=seq_len // tm, sb=sb,
                          n_experts=n_experts),
        grid=(n // tm,),
        in_specs=in_specs,
        out_specs=out_specs,
        out_shape=out_shape,
        compiler_params=pltpu.CompilerParams(
            dimension_semantics=("arbitrary",), vmem_limit_bytes=VMEM_LIMIT_BYTES),
        name="mix_out_moe" if n_experts else "mix_out",
    )(*args)


def _swiglu_step(x, w1_ref, w3_ref, w2_ref):
    a = _dot(x, w1_ref[0])
    hidden = a * jax.nn.sigmoid(a) * _dot(x, w3_ref[0])
    return _dot(hidden.astype(BF16), w2_ref[0])


def _ple(h2, p, pg, wg, wp):
    gate = jax.nn.sigmoid(_dot(_rms(h2, pg).astype(BF16), wg))
    return h2 + gate * _dot(p.astype(BF16), wp)


def _ffn_kernel(f_ref, w1_ref, w3_ref, w2_ref, h1_ref, p_ref, pg_ref, wg_ref, wp_ref,
                o_ref, acc_ref):
    c = pl.program_id(1)

    @pl.when(c == 0)
    def _():
        acc_ref[...] = jnp.zeros_like(acc_ref)

    acc_ref[...] += _swiglu_step(f_ref[...], w1_ref, w3_ref, w2_ref)

    @pl.when(c == pl.num_programs(1) - 1)
    def _():
        o_ref[...] = _ple(h1_ref[...] + acc_ref[...], p_ref[...], pg_ref[...], wg_ref[...],
                          wp_ref[...])


def _ffn(f, w1, w3, w2, h1, p, pg, wg, wp, *, tm, tf):
    n, d = h1.shape
    d_ff = w1.shape[2]
    row = lambda i, c: (i, 0)
    fixed = lambda i, c: (0, 0)
    in_specs = [
        pl.BlockSpec((tm, d), row),
        pl.BlockSpec((1, d, tf), lambda i, c: (0, 0, c)),
        pl.BlockSpec((1, d, tf), lambda i, c: (0, 0, c)),
        pl.BlockSpec((1, tf, d), lambda i, c: (0, c, 0)),
        pl.BlockSpec((tm, d), row),
        pl.BlockSpec((tm, p.shape[1]), row),
        pl.BlockSpec((1, d), fixed),
        pl.BlockSpec(wg.shape, fixed),
        pl.BlockSpec(wp.shape, fixed),
    ]
    return pl.pallas_call(
        _ffn_kernel,
        grid=(n // tm, d_ff // tf),
        in_specs=in_specs,
        out_specs=pl.BlockSpec((tm, d), row),
        out_shape=jax.ShapeDtypeStruct((n, d), F32),
        scratch_shapes=[pltpu.VMEM((tm, d), F32)],
        compiler_params=pltpu.CompilerParams(
            dimension_semantics=("arbitrary", "arbitrary"),
            vmem_limit_bytes=VMEM_LIMIT_BYTES),
        name="ffn_dense",
    )(f, w1, w3, w2, h1, p, pg, wg, wp)


def _route_plan(route, n_experts, tm):
    n = route.shape[1]
    pairs = TOP_K * n
    n_rows = pairs + n_experts * tm
    eid = route[0:TOP_K].T.astype(jnp.int32).reshape(pairs)
    gates = route[TOP_K:2 * TOP_K].T.reshape(pairs)
    pair_bits = int(np.ceil(np.log2(pairs)))
    keys = jnp.sort(eid * (1 << pair_bits) + jnp.arange(pairs, dtype=jnp.int32))
    order = keys & ((1 << pair_bits) - 1)
    bounds = jnp.arange(n_experts + 1, dtype=jnp.int32) << pair_bits
    bounds = bounds.at[n_experts].set(jnp.iinfo(jnp.int32).max)
    first = jnp.sum(keys[None, :] < bounds[:, None], axis=1).astype(jnp.int32)
    counts = first[1:] - first[:-1]
    padded = (counts + tm - 1) // tm * tm
    ends = jnp.cumsum(padded)
    n_tiles = n_rows // tm
    tile_start = jnp.arange(n_tiles, dtype=jnp.int32) * tm
    tile_expert = jnp.sum(tile_start[:, None] >= ends[None, :], axis=1)
    tile_expert = jnp.minimum(tile_expert, n_experts - 1).astype(jnp.int32)
    in_group = tile_start - (ends - padded)[tile_expert]
    tile_rows = jnp.clip(counts[tile_expert] - in_group, 0, tm)
    lane = jnp.arange(tm, dtype=jnp.int32)[None, :]
    real = lane < tile_rows[:, None]
    safe = order[jnp.clip((first[tile_expert] + in_group)[:, None] + lane, 0, pairs - 1)]
    src_token = jnp.where(real, safe // TOP_K, 0)
    gate_row = jnp.where(real, gates[safe], 0.0)
    spare = pairs + (jnp.arange(n_tiles, dtype=jnp.int32) % 2 * tm)[:, None] + lane
    dst_row = jnp.where(real, (safe % TOP_K) * n + safe // TOP_K, spare)
    src_token = jnp.concatenate([src_token, jnp.zeros((1, tm), jnp.int32)])
    dst_row = jnp.concatenate([pairs + tm + lane, dst_row])
    return (src_token.reshape(n_tiles + 1, 1, tm), dst_row.reshape(n_tiles + 1, 1, tm),
            gate_row.reshape(n_rows, 1), tile_expert, tile_rows)


def _for_each_row(count, fn, unroll=8):
    if isinstance(count, int):
        full = count // unroll
    else:
        full = lax.shift_right_logical(count, jnp.int32(int(np.log2(unroll))))

    def chunk(i, carry):
        for u in range(unroll):
            fn(i * unroll + u)
        return carry

    def single(r, carry):
        fn(r)
        return carry

    lax.fori_loop(0, full, chunk, 0)
    lax.fori_loop(full * unroll, count, single, 0)


def _moe_ffn_kernel(expert_ref, count_ref, src0_ref, src_ref, dst_ref, f_hbm, gate_ref,
                    w1_ref, w3_ref, w2_ref, y_hbm, rows_ref, x_ref, acc_ref, stage_ref,
                    gather_sem, scatter_sem, *, tm, n_chunks, n_pairs):
    j = pl.program_id(0)
    c = pl.program_id(1)
    last_tile = pl.num_programs(0) - 1
    last_chunk = n_chunks - 1
    per_step = tm // n_chunks
    active = count_ref[j] > 0
    cur = j % 2
    other = 1 - cur

    def tile_gather(slot):
        return pltpu.make_async_copy(f_hbm.at[pl.ds(0, tm), :], rows_ref.at[slot],
                                     gather_sem.at[slot])

    def tile_scatter(slot):
        return pltpu.make_async_copy(stage_ref.at[slot], y_hbm.at[pl.ds(0, tm), :],
                                     scatter_sem.at[slot])

    @pl.when((j == 0) & (c == 0))
    def _():
        stage_ref[...] = jnp.zeros_like(stage_ref)
        for slot in range(2):
            spare = pltpu.make_async_copy(stage_ref.at[slot],
                                          y_hbm.at[pl.ds(n_pairs + slot * tm, tm), :],
                                          scatter_sem.at[slot])
            spare.start()
            spare.wait()

        def gather_first(r):
            pltpu.make_async_copy(f_hbm.at[pl.ds(src0_ref[0, 0, r], 1), :],
                                  rows_ref.at[0, pl.ds(r, 1), :], gather_sem.at[0]).start()
        _for_each_row(tm, gather_first)

    @pl.when(c == 0)
    def _():
        tile_gather(cur).wait()

        @pl.when(j >= 1)
        def _():
            tile_scatter(cur).wait()

        x_ref[...] = rows_ref[cur].astype(BF16)
        acc_ref[...] = jnp.zeros_like(acc_ref)

    def start_row_copies():
        base = pl.multiple_of(c * per_step, per_step)
        for r in range(per_step):
            pltpu.make_async_copy(f_hbm.at[pl.ds(src_ref[0, 0, base + r], 1), :],
                                  rows_ref.at[other, pl.ds(base + r, 1), :],
                                  gather_sem.at[other]).start()
            pltpu.make_async_copy(stage_ref.at[other, pl.ds(base + r, 1), :],
                                  y_hbm.at[pl.ds(dst_ref[0, 0, base + r], 1), :],
                                  scatter_sem.at[other]).start()

    @pl.when(active)
    def _():
        start_row_copies()
        acc_ref[...] += _swiglu_step(x_ref[...], w1_ref, w3_ref, w2_ref)

    @pl.when(jnp.logical_not(active))
    def _():
        start_row_copies()

    @pl.when(active & (c == last_chunk))
    def _():
        stage_ref[cur] = acc_ref[...] * gate_ref[...]

    @pl.when((j == last_tile) & (c == last_chunk))
    def _():
        tile_gather(other).wait()
        tile_scatter(other).wait()


def _moe_ffn(f, plan, w1, w3, w2, *, tm, tf):
    src_token, dst_row, gate_row, tile_expert, tile_rows = plan
    n, d = f.shape
    n_exp, _, d_ff = w1.shape
    n_tiles = tile_rows.shape[0]
    n_chunks = d_ff // tf
    n_pairs = TOP_K * n
    assert n_pairs % tm == 0 and tm % n_chunks == 0 and n_exp < tm
    chunk = lambda j, c, count: jnp.where(count[j] > 0, c, n_chunks - 1)
    grid_spec = pltpu.PrefetchScalarGridSpec(
        num_scalar_prefetch=2,
        grid=(n_tiles, n_chunks),
        in_specs=[
            pl.BlockSpec((1, 1, tm), lambda j, c, ex, count: (0, 0, 0), memory_space=pltpu.SMEM),
            pl.BlockSpec((1, 1, tm), lambda j, c, ex, count: (j + 1, 0, 0),
                         memory_space=pltpu.SMEM),
            pl.BlockSpec((1, 1, tm), lambda j, c, ex, count: (j, 0, 0), memory_space=pltpu.SMEM),
            pl.BlockSpec(memory_space=pl.ANY),
            pl.BlockSpec((tm, 1), lambda j, c, ex, count: (j, 0)),
            pl.BlockSpec((1, d, tf), lambda j, c, ex, count: (ex[j], 0, chunk(j, c, count))),
            pl.BlockSpec((1, d, tf), lambda j, c, ex, count: (ex[j], 0, chunk(j, c, count))),
            pl.BlockSpec((1, tf, d), lambda j, c, ex, count: (ex[j], chunk(j, c, count), 0)),
        ],
        out_specs=pl.BlockSpec(memory_space=pl.ANY),
        scratch_shapes=[pltpu.VMEM((2, tm, d), F32), pltpu.VMEM((tm, d), BF16),
                        pltpu.VMEM((tm, d), F32), pltpu.VMEM((2, tm, d), F32),
                        pltpu.SemaphoreType.DMA((2,)), pltpu.SemaphoreType.DMA((2,))],
    )
    return pl.pallas_call(
        functools.partial(_moe_ffn_kernel, tm=tm, n_chunks=n_chunks, n_pairs=n_pairs),
        grid_spec=grid_spec,
        out_shape=jax.ShapeDtypeStruct((n_pairs + 2 * tm, d), F32),
        compiler_params=pltpu.CompilerParams(
            dimension_semantics=("arbitrary", "arbitrary"),
            vmem_limit_bytes=VMEM_LIMIT_BYTES),
        name="ffn_moe",
    )(tile_expert, tile_rows, src_token, src_token, dst_row, f, gate_row, w1, w3, w2)


def _ple_kernel(h1_ref, y0_ref, y1_ref, p_ref, pg_ref, wg_ref, wp_ref, o_ref):
    h2 = h1_ref[...] + y0_ref[...] + y1_ref[...]
    o_ref[...] = _ple(h2, p_ref[...], pg_ref[...], wg_ref[...], wp_ref[...])


def _ple_combine(h1, y, p, pg, wg, wp, *, tm):
    n, d = h1.shape
    row = lambda i: (i, 0)
    fixed = lambda i: (0, 0)
    return pl.pallas_call(
        _ple_kernel,
        grid=(n // tm,),
        in_specs=[
            pl.BlockSpec((tm, d), row),
            pl.BlockSpec((tm, d), row),
            pl.BlockSpec((tm, d), lambda i: (i + n // tm, 0)),
            pl.BlockSpec((tm, p.shape[1]), row),
            pl.BlockSpec((1, d), fixed),
            pl.BlockSpec(wg.shape, fixed),
            pl.BlockSpec(wp.shape, fixed),
        ],
        out_specs=pl.BlockSpec((tm, d), row),
        out_shape=jax.ShapeDtypeStruct((n, d), F32),
        compiler_params=pltpu.CompilerParams(
            dimension_semantics=("arbitrary",), vmem_limit_bytes=VMEM_LIMIT_BYTES),
        name="ple_combine",
    )(h1, y, y, p, pg, wg, wp)


def kernel(x, p, mix_norm_g, w_in, q_norm_g, k_norm_g, conv_w, conv_b, attn_out_g, conv_out_g,
           w_o, ffn_norm_g, dense_w1, dense_w3, dense_w2, router_w, moe_w1, moe_w3, moe_w2,
           ple_norm_g, ple_gate_w, ple_proj_w):
    bsz, seq_len, d = x.shape
    depth = w_in.shape[0]
    sb = attn_out_g.shape[1]
    n_heads = sb // HEAD_DIM
    n_experts = router_w.shape[2]
    n = bsz * seq_len
    tm = min(ROW_TILE, seq_len)
    t = min(ATTN_TILE, seq_len)
    tm_ffn = min(FFN_ROW_TILE, n)

    head_id = jnp.arange(sb) // HEAD_DIM
    head_mean = jnp.where(head_id[:, None] == head_id[None, :], 1.0 / HEAD_DIM, 0.0).astype(BF16)
    ar = jnp.arange(t)
    tri = (ar[:, None] >= ar[None, :]).astype(BF16)
    q_scale = LOG2E / float(np.sqrt(HEAD_DIM))
    row2 = lambda a: a.reshape(1, -1)

    h = x.reshape(n, d)
    for i in range(depth):
        qg = row2(jnp.tile(q_norm_g[i], n_heads) * q_scale)
        kg = row2(jnp.tile(k_norm_g[i], n_heads))
        q, k, v, cu, b = _mix_in(h, row2(mix_norm_g[i]), w_in[i].astype(BF16), qg, kg,
                                 head_mean, tm=tm, seq_len=seq_len)
        attn = _attention(q, k, v, tri, t=t, pairs=ATTN_PAIRS)
        j = i // 2
        routed = i % 2 == 1
        rw = None
        if routed:
            e_pad = -(-n_experts // SUBLANES) * SUBLANES
            rw = jnp.pad(router_w[j].T, ((0, e_pad - n_experts), (0, 0)))
            rw_hi = rw.astype(BF16)
            rw = jnp.stack([rw_hi, (rw - rw_hi.astype(F32)).astype(BF16)])
        outs = _mix_out(attn, cu, b, h, row2(attn_out_g[i]), row2(conv_out_g[i]), conv_w[i],
                        row2(conv_b[i]), w_o[i].astype(BF16), row2(ffn_norm_g[i]), rw,
                        tm=tm, seq_len=seq_len, n_experts=n_experts if routed else 0)
        pi = p[i].reshape(n, -1)
        ple_w = (row2(ple_norm_g[i]), ple_gate_w[i].astype(BF16), ple_proj_w[i].astype(BF16))
        if routed:
            h1, f, route = outs
            plan = _route_plan(route, n_experts, MOE_ROW_TILE)
            y = _moe_ffn(f, plan, moe_w1[j].astype(BF16), moe_w3[j].astype(BF16),
                         moe_w2[j].astype(BF16), tm=MOE_ROW_TILE, tf=MOE_COL_TILE)
            h = _ple_combine(h1, y, pi, *ple_w, tm=tm)
        else:
            h1, f = outs
            h = _ffn(f, dense_w1[j:j + 1].astype(BF16), dense_w3[j:j + 1].astype(BF16),
                     dense_w2[j:j + 1].astype(BF16), h1, pi, *ple_w, tm=tm_ffn,
                     tf=FFN_COL_TILE)
    return h.reshape(bsz, seq_len, d)
```

```python
import functools

import numpy as np
import jax
import jax.numpy as jnp
from jax import lax
from jax.experimental import pallas as pl
from jax.experimental.pallas import tpu as pltpu

F32 = jnp.float32
BF16 = jnp.bfloat16

EPS = 1e-6
HEAD_DIM = 64
CONV_K = 3
TOP_K = 2
LANES = 128
SUBLANES = 8
HALO_ROWS = SUBLANES
LOG2E = float(np.log2(np.e))

VMEM_LIMIT_BYTES = 56 * 1024 * 1024

ROW_TILE = 512
ATTN_TILE = 256
ATTN_PAIRS = 4
SP_CLAMP = 64.0
DONE_LOG2 = 160.0
NO_BLOCK_DECAY = 1e30
FFN_ROW_TILE = 512
FFN_COL_TILE = 1792
MOE_ROW_TILE = 512
MOE_COL_TILE = 1792


def _rms(x, g):
    return x * lax.rsqrt(jnp.mean(x * x, axis=-1, keepdims=True) + EPS) * g


def _dot(a, b):
    return jnp.dot(a, b, preferred_element_type=F32)


def _mix_in_kernel(h_ref, g_ref, w_ref, qg_ref, kg_ref, hm_ref,
                   q_ref, k_ref, v_ref, cu_ref, b_ref, *, sb):
    a = _rms(h_ref[...], g_ref[...]).astype(BF16)
    hm = hm_ref[...]

    def qk_norm(t, g):
        ms = _dot((t * t).astype(BF16), hm)
        return (t * lax.rsqrt(ms + EPS) * g).astype(BF16)

    q_ref[0] = qk_norm(_dot(a, w_ref[:, 0:sb]), qg_ref[...])
    k_ref[0] = qk_norm(_dot(a, w_ref[:, sb:2 * sb]), kg_ref[...])
    v_ref[0] = _dot(a, w_ref[:, 2 * sb:3 * sb]).astype(BF16)
    u = _dot(a, w_ref[:, 3 * sb:4 * sb])
    c = _dot(a, w_ref[:, 4 * sb:5 * sb])
    cu_ref[...] = c * u
    b_ref[...] = _dot(a, w_ref[:, 5 * sb:6 * sb])


def _mix_in(h, g, w_in, qg, kg, head_mean, *, tm, seq_len):
    n, d = h.shape
    sb = head_mean.shape[0]
    tiles_per_seq = seq_len // tm
    row = lambda i: (i, 0)
    fixed = lambda i: (0, 0)
    seq_row = lambda i: (i // tiles_per_seq, i % tiles_per_seq, 0)
    out_bf = jax.ShapeDtypeStruct((n // seq_len, seq_len, sb), BF16)
    out_f32 = jax.ShapeDtypeStruct((n, sb), F32)
    return pl.pallas_call(
        functools.partial(_mix_in_kernel, sb=sb),
        grid=(n // tm,),
        in_specs=[
            pl.BlockSpec((tm, d), row),
            pl.BlockSpec((1, d), fixed),
            pl.BlockSpec(w_in.shape, fixed),
            pl.BlockSpec((1, sb), fixed),
            pl.BlockSpec((1, sb), fixed),
            pl.BlockSpec((sb, sb), fixed),
        ],
        out_specs=[pl.BlockSpec((1, tm, sb), seq_row)] * 3 + [pl.BlockSpec((tm, sb), row)] * 2,
        out_shape=[out_bf, out_bf, out_bf, out_f32, out_f32],
        compiler_params=pltpu.CompilerParams(
            dimension_semantics=("arbitrary",), vmem_limit_bytes=VMEM_LIMIT_BYTES),
        name="mix_in",
    )(h, g, w_in, qg, kg, head_mean)


def _attn_kernel(q_ref, k_ref, v_ref, u_ref, o_ref, acc_ref, c_ref, *, t, pairs):
    i = pl.program_id(2)
    heads = 2 * pairs
    q = q_ref[0]
    lane = lax.broadcasted_iota(jnp.int32, (t, LANES), 1)
    lo_half = lane < HEAD_DIM
    q_heads = []
    for p in range(pairs):
        qp = q[:, p * LANES:(p + 1) * LANES]
        q_heads += [jnp.where(lo_half, qp, jnp.zeros_like(qp)),
                    jnp.where(lo_half, jnp.zeros_like(qp), qp)]
    tri = u_ref[...]
    row = lax.broadcasted_iota(jnp.int32, (t, t), 0)
    col = lax.broadcasted_iota(jnp.int32, (t, t), 1)
    causal = col < row

    acc_ref[...] = jnp.zeros_like(acc_ref)
    c_ref[...] = jnp.zeros_like(c_ref)

    def block(j, diagonal, extra_decay=0.0):
        start = pl.multiple_of(j * t, t)
        kj = k_ref[0, pl.ds(start, t), :]
        vj = v_ref[0, pl.ds(start, t), :]
        for hd in range(heads):
            p = hd // 2
            z = lax.dot_general(q_heads[hd], kj[:, p * LANES:(p + 1) * LANES],
                                (((1,), (1,)), ((), ())), preferred_element_type=F32)
            sp = jnp.maximum(z, jnp.log2(1.0 + jnp.exp2(jnp.minimum(z, SP_CLAMP))))
            if diagonal:
                sp = jnp.where(causal, sp, 0.0)
            incl = _dot(sp.astype(BF16), tri)
            w = jnp.exp2(z - incl - (c_ref[hd] + extra_decay))
            if diagonal:
                w = jnp.where(causal, w, 0.0)
            acc_ref[hd] += _dot(w.astype(BF16), vj[:, p * LANES:(p + 1) * LANES])
            c_ref[hd] += incl[:, 0:1]

    def decay_floor():
        m = jnp.min(c_ref[0])
        for hd in range(1, heads):
            m = jnp.minimum(m, jnp.min(c_ref[hd]))
        return m

    block(i, True)
    block(jnp.maximum(i - 1, 0), False, extra_decay=jnp.where(i > 0, 0.0, NO_BLOCK_DECAY))

    def cond(carry):
        j, floor = carry
        return (j >= 0) & (floor < DONE_LOG2)

    def body(carry):
        j, _ = carry
        block(j, False)
        return j - 1, decay_floor()

    lax.while_loop(cond, body, (i - 2, decay_floor()))
    for p in range(pairs):
        o_ref[0, :, p * LANES:(p + 1) * LANES] = jnp.where(
            lo_half, acc_ref[2 * p], acc_ref[2 * p + 1]).astype(o_ref.dtype)


def _attention(q, k, v, tri, *, t, pairs):
    bsz, s, sb = q.shape
    width = LANES * pairs
    blk = lambda b, g, i: (b, i, g)
    seq = lambda b, g, i: (b, 0, g)
    return pl.pallas_call(
        functools.partial(_attn_kernel, t=t, pairs=pairs),
        grid=(bsz, sb // width, s // t),
        in_specs=[
            pl.BlockSpec((1, t, width), blk),
            pl.BlockSpec((1, s, width), seq),
            pl.BlockSpec((1, s, width), seq),
            pl.BlockSpec((t, t), lambda b, g, i: (0, 0)),
        ],
        out_specs=pl.BlockSpec((1, t, width), blk),
        out_shape=jax.ShapeDtypeStruct((bsz, s, sb), BF16),
        scratch_shapes=[pltpu.VMEM((2 * pairs, t, LANES), F32),
                        pltpu.VMEM((2 * pairs, t, 1), F32)],
        compiler_params=pltpu.CompilerParams(
            dimension_semantics=("arbitrary", "arbitrary", "arbitrary"),
            vmem_limit_bytes=VMEM_LIMIT_BYTES),
        name="attention",
    )(q, k, v, tri)


def _mix_out_kernel(*refs, tiles_per_seq, sb, n_experts):
    (attn_ref, cu_ref, halo_ref, b_ref, h_ref, ag_ref, cg_ref, cw_ref, cb_ref,
     wo_ref, fg_ref) = refs[:11]
    if n_experts:
        rw_ref, h1_ref, f_ref, route_ref = refs[11:]
    else:
        h1_ref, f_ref = refs[11:]

    attn_n = _rms(attn_ref[0].astype(F32), ag_ref[...])

    cu = cu_ref[...]
    first = pl.program_id(0) % tiles_per_seq == 0
    halo = jnp.where(first, 0.0, halo_ref[...])
    rowi = lax.broadcasted_iota(jnp.int32, cu.shape, 0)
    prev1 = jnp.where(rowi == 0, halo[HALO_ROWS - 1:HALO_ROWS], pltpu.roll(cu, 1, axis=0))
    prev2 = jnp.where(rowi == 0, halo[HALO_ROWS - 2:HALO_ROWS - 1],
                      jnp.where(rowi == 1, halo[HALO_ROWS - 1:HALO_ROWS],
                                pltpu.roll(cu, 2, axis=0)))
    conv = cw_ref[0:1] * prev2 + cw_ref[1:2] * prev1 + cw_ref[2:3] * cu + cb_ref[...]
    conv_n = _rms(b_ref[...] * conv, cg_ref[...])

    h1 = (h_ref[...] + _dot(attn_n.astype(BF16), wo_ref[0:sb])
          + _dot(conv_n.astype(BF16), wo_ref[sb:]))
    h1_ref[...] = h1
    f = _rms(h1, fg_ref[...])
    f_ref[...] = f.astype(f_ref.dtype)

    if n_experts:
        f_hi = f.astype(BF16)
        f_lo = (f - f_hi.astype(F32)).astype(BF16)
        nt = (((1,), (1,)), ((), ()))
        logits = (lax.dot_general(rw_ref[0], f_hi, nt, preferred_element_type=F32)
                  + (lax.dot_general(rw_ref[0], f_lo, nt, preferred_element_type=F32)
                     + lax.dot_general(rw_ref[1], f_hi, nt, preferred_element_type=F32)))
        e_pad = logits.shape[0]
        eid = lax.broadcasted_iota(jnp.int32, logits.shape, 0).astype(F32)
        neg = jnp.float32(-jnp.inf)
        logits = jnp.where(eid < n_experts, logits, neg)
        m1 = jnp.max(logits, axis=0, keepdims=True)
        i1 = jnp.min(jnp.where(logits == m1, eid, float(e_pad)), axis=0, keepdims=True)
        rest = jnp.where(eid == i1, neg, logits)
        m2 = jnp.max(rest, axis=0, keepdims=True)
        i2 = jnp.min(jnp.where(rest == m2, eid, float(e_pad)), axis=0, keepdims=True)
        e2 = jnp.exp(m2 - m1)
        g1 = 1.0 / (1.0 + e2)
        g2 = e2 * g1
        slot = lax.broadcasted_iota(jnp.int32, route_ref.shape, 0)
        route_ref[...] = jnp.where(
            slot == 0, i1, jnp.where(slot == 1, i2, jnp.where(
                slot == 2, g1, jnp.where(slot == 3, g2, 0.0))))


def _mix_out(attn, cu, b, h, ag, cg, cw, cb, w_o, fg, router_w, *, tm, seq_len, n_experts):
    n, d = h.shape
    sb = attn.shape[2]
    tiles_per_seq = seq_len // tm
    row = lambda i: (i, 0)
    fixed = lambda i: (0, 0)
    halo = lambda i: (jnp.maximum(i * (tm // HALO_ROWS) - 1, 0), 0)
    in_specs = [
        pl.BlockSpec((1, tm, sb), lambda i: (i // tiles_per_seq, i % tiles_per_seq, 0)),
        pl.BlockSpec((tm, sb), row),
        pl.BlockSpec((HALO_ROWS, sb), halo),
        pl.BlockSpec((tm, sb), row),
        pl.BlockSpec((tm, d), row),
        pl.BlockSpec((1, sb), fixed),
        pl.BlockSpec((1, sb), fixed),
        pl.BlockSpec((CONV_K, sb), fixed),
        pl.BlockSpec((1, sb), fixed),
        pl.BlockSpec(w_o.shape, fixed),
        pl.BlockSpec((1, d), fixed),
    ]
    args = [attn, cu, cu, b, h, ag, cg, cw, cb, w_o, fg]
    out_specs = [pl.BlockSpec((tm, d), row), pl.BlockSpec((tm, d), row)]
    out_shape = [jax.ShapeDtypeStruct((n, d), F32),
                 jax.ShapeDtypeStruct((n, d), F32 if n_experts else BF16)]
    if n_experts:
        in_specs.append(pl.BlockSpec(router_w.shape, lambda i: (0, 0, 0)))
        args.append(router_w)
        out_specs.append(pl.BlockSpec((SUBLANES, tm), lambda i: (0, i)))
        out_shape.append(jax.ShapeDtypeStruct((SUBLANES, n), F32))
    return pl.pallas_call(
        functools.partial(_mix_out_kernel, tiles_per_seq=seq_len // tm, sb=sb,
                          n_experts=n_experts),
        grid=(n // tm,),
        in_specs=in_specs,
        out_specs=out_specs,
        out_shape=out_shape,
        compiler_params=pltpu.CompilerParams(
            dimension_semantics=("arbitrary",), vmem_limit_bytes=VMEM_LIMIT_BYTES),
        name="mix_out_moe" if n_experts else "mix_out",
    )(*args)


def _swiglu_step(x, w1_ref, w3_ref, w2_ref):
    a = _dot(x, w1_ref[0])
    hidden = a * jax.nn.sigmoid(a) * _dot(x, w3_ref[0])
    return _dot(hidden.astype(BF16), w2_ref[0])


def _ple(h2, p, pg, wg, wp):
    gate = jax.nn.sigmoid(_dot(_rms(h2, pg).astype(BF16), wg))
    return h2 + gate * _dot(p.astype(BF16), wp)


def _ffn_kernel(f_ref, w1_ref, w3_ref, w2_ref, h1_ref, p_ref, pg_ref, wg_ref, wp_ref,
                o_ref, acc_ref):
    c = pl.program_id(1)

    @pl.when(c == 0)
    def _():
        acc_ref[...] = jnp.zeros_like(acc_ref)

    acc_ref[...] += _swiglu_step(f_ref[...], w1_ref, w3_ref, w2_ref)

    @pl.when(c == pl.num_programs(1) - 1)
    def _():
        o_ref[...] = _ple(h1_ref[...] + acc_ref[...], p_ref[...], pg_ref[...], wg_ref[...],
                          wp_ref[...])


def _ffn(f, w1, w3, w2, h1, p, pg, wg, wp, *, tm, tf):
    n, d = h1.shape
    d_ff = w1.shape[2]
    row = lambda i, c: (i, 0)
    fixed = lambda i, c: (0, 0)
    in_specs = [
        pl.BlockSpec((tm, d), row),
        pl.BlockSpec((1, d, tf), lambda i, c: (0, 0, c)),
        pl.BlockSpec((1, d, tf), lambda i, c: (0, 0, c)),
        pl.BlockSpec((1, tf, d), lambda i, c: (0, c, 0)),
        pl.BlockSpec((tm, d), row),
        pl.BlockSpec((tm, p.shape[1]), row),
        pl.BlockSpec((1, d), fixed),
        pl.BlockSpec(wg.shape, fixed),
        pl.BlockSpec(wp.shape, fixed),
    ]
    return pl.pallas_call(
        _ffn_kernel,
        grid=(n // tm, d_ff // tf),
        in_specs=in_specs,
        out_specs=pl.BlockSpec((tm, d), row),
        out_shape=jax.ShapeDtypeStruct((n, d), F32),
        scratch_shapes=[pltpu.VMEM((tm, d), F32)],
        compiler_params=pltpu.CompilerParams(
            dimension_semantics=("arbitrary", "arbitrary"),
            vmem_limit_bytes=VMEM_LIMIT_BYTES),
        name="ffn_dense",
    )(f, w1, w3, w2, h1, p, pg, wg, wp)


def _route_plan(route, n_experts, tm):
    n = route.shape[1]
    pairs = TOP_K * n
    n_rows = pairs + n_experts * tm
    eid = route[0:TOP_K].T.astype(jnp.int32).reshape(pairs)
    gates = route[TOP_K:2 * TOP_K].T.reshape(pairs)
    pair_bits = int(np.ceil(np.log2(pairs)))
    keys = jnp.sort(eid * (1 << pair_bits) + jnp.arange(pairs, dtype=jnp.int32))
    order = keys & ((1 << pair_bits) - 1)
    bounds = jnp.arange(n_experts + 1, dtype=jnp.int32) << pair_bits
    bounds = bounds.at[n_experts].set(jnp.iinfo(jnp.int32).max)
    first = jnp.sum(keys[None, :] < bounds[:, None], axis=1).astype(jnp.int32)
    counts = first[1:] - first[:-1]
    padded = (counts + tm - 1) // tm * tm
    ends = jnp.cumsum(padded)
    n_tiles = n_rows // tm
    tile_start = jnp.arange(n_tiles, dtype=jnp.int32) * tm
    tile_expert = jnp.sum(tile_start[:, None] >= ends[None, :], axis=1)
    tile_expert = jnp.minimum(tile_expert, n_experts - 1).astype(jnp.int32)
    in_group = tile_start - (ends - padded)[tile_expert]
    tile_rows = jnp.clip(counts[tile_expert] - in_group, 0, tm)
    lane = jnp.arange(tm, dtype=jnp.int32)[None, :]
    real = lane < tile_rows[:, None]
    safe = order[jnp.clip((first[tile_expert] + in_group)[:, None] + lane, 0, pairs - 1)]
    src_token = jnp.where(real, safe // TOP_K, 0)
    gate_row = jnp.where(real, gates[safe], 0.0)
    spare = pairs + (jnp.arange(n_tiles, dtype=jnp.int32) % 2 * tm)[:, None] + lane
    dst_row = jnp.where(real, (safe % TOP_K) * n + safe // TOP_K, spare)
    src_token = jnp.concatenate([src_token, jnp.zeros((1, tm), jnp.int32)])
    dst_row = jnp.concatenate([pairs + tm + lane, dst_row])
    return (src_token.reshape(n_tiles + 1, 1, tm), dst_row.reshape(n_tiles + 1, 1, tm),
            gate_row.reshape(n_rows, 1), tile_expert, tile_rows)


def _for_each_row(count, fn, unroll=8):
    if isinstance(count, int):
        full = count // unroll
    else:
        full = lax.shift_right_logical(count, jnp.int32(int(np.log2(unroll))))

    def chunk(i, carry):
        for u in range(unroll):
            fn(i * unroll + u)
        return carry

    def single(r, carry):
        fn(r)
        return carry

    lax.fori_loop(0, full, chunk, 0)
    lax.fori_loop(full * unroll, count, single, 0)


def _moe_ffn_kernel(expert_ref, count_ref, src0_ref, src_ref, dst_ref, f_hbm, gate_ref,
                    w1_ref, w3_ref, w2_ref, y_hbm, rows_ref, x_ref, acc_ref, stage_ref,
                    gather_sem, scatter_sem, *, tm, n_chunks, n_pairs):
    j = pl.program_id(0)
    c = pl.program_id(1)
    last_tile = pl.num_programs(0) - 1
    last_chunk = n_chunks - 1
    per_step = tm // n_chunks
    active = count_ref[j] > 0
    cur = j % 2
    other = 1 - cur

    def tile_gather(slot):
        return pltpu.make_async_copy(f_hbm.at[pl.ds(0, tm), :], rows_ref.at[slot],
                                     gather_sem.at[slot])

    def tile_scatter(slot):
        return pltpu.make_async_copy(stage_ref.at[slot], y_hbm.at[pl.ds(0, tm), :],
                                     scatter_sem.at[slot])

    @pl.when((j == 0) & (c == 0))
    def _():
        stage_ref[...] = jnp.zeros_like(stage_ref)
        for slot in range(2):
            spare = pltpu.make_async_copy(stage_ref.at[slot],
                                          y_hbm.at[pl.ds(n_pairs + slot * tm, tm), :],
                                          scatter_sem.at[slot])
            spare.start()
            spare.wait()

        def gather_first(r):
            pltpu.make_async_copy(f_hbm.at[pl.ds(src0_ref[0, 0, r], 1), :],
                                  rows_ref.at[0, pl.ds(r, 1), :], gather_sem.at[0]).start()
        _for_each_row(tm, gather_first)

    @pl.when(c == 0)
    def _():
        tile_gather(cur).wait()

        @pl.when(j >= 1)
        def _():
            tile_scatter(cur).wait()

        x_ref[...] = rows_ref[cur].astype(BF16)
        acc_ref[...] = jnp.zeros_like(acc_ref)

    def start_row_copies():
        base = pl.multiple_of(c * per_step, per_step)
        for r in range(per_step):
            pltpu.make_async_copy(f_hbm.at[pl.ds(src_ref[0, 0, base + r], 1), :],
                                  rows_ref.at[other, pl.ds(base + r, 1), :],
                                  gather_sem.at[other]).start()
            pltpu.make_async_copy(stage_ref.at[other, pl.ds(base + r, 1), :],
                                  y_hbm.at[pl.ds(dst_ref[0, 0, base + r], 1), :],
                                  scatter_sem.at[other]).start()

    @pl.when(active)
    def _():
        start_row_copies()
        acc_ref[...] += _swiglu_step(x_ref[...], w1_ref, w3_ref, w2_ref)

    @pl.when(jnp.logical_not(active))
    def _():
        start_row_copies()

    @pl.when(active & (c == last_chunk))
    def _():
        stage_ref[cur] = acc_ref[...] * gate_ref[...]

    @pl.when((j == last_tile) & (c == last_chunk))
    def _():
        tile_gather(other).wait()
        tile_scatter(other).wait()


def _moe_ffn(f, plan, w1, w3, w2, *, tm, tf):
    src_token, dst_row, gate_row, tile_expert, tile_rows = plan
    n, d = f.shape
    n_exp, _, d_ff = w1.shape
    n_tiles = tile_rows.shape[0]
    n_chunks = d_ff // tf
    n_pairs = TOP_K * n
    assert n_pairs % tm == 0 and tm % n_chunks == 0 and n_exp < tm
    chunk = lambda j, c, count: jnp.where(count[j] > 0, c, n_chunks - 1)
    grid_spec = pltpu.PrefetchScalarGridSpec(
        num_scalar_prefetch=2,
        grid=(n_tiles, n_chunks),
        in_specs=[
            pl.BlockSpec((1, 1, tm), lambda j, c, ex, count: (0, 0, 0), memory_space=pltpu.SMEM),
            pl.BlockSpec((1, 1, tm), lambda j, c, ex, count: (j + 1, 0, 0),
                         memory_space=pltpu.SMEM),
            pl.BlockSpec((1, 1, tm), lambda j, c, ex, count: (j, 0, 0), memory_space=pltpu.SMEM),
            pl.BlockSpec(memory_space=pl.ANY),
            pl.BlockSpec((tm, 1), lambda j, c, ex, count: (j, 0)),
            pl.BlockSpec((1, d, tf), lambda j, c, ex, count: (ex[j], 0, chunk(j, c, count))),
            pl.BlockSpec((1, d, tf), lambda j, c, ex, count: (ex[j], 0, chunk(j, c, count))),
            pl.BlockSpec((1, tf, d), lambda j, c, ex, count: (ex[j], chunk(j, c, count), 0)),
        ],
        out_specs=pl.BlockSpec(memory_space=pl.ANY),
        scratch_shapes=[pltpu.VMEM((2, tm, d), F32), pltpu.VMEM((tm, d), BF16),
                        pltpu.VMEM((tm, d), F32), pltpu.VMEM((2, tm, d), F32),
                        pltpu.SemaphoreType.DMA((2,)), pltpu.SemaphoreType.DMA((2,))],
    )
    return pl.pallas_call(
        functools.partial(_moe_ffn_kernel, tm=tm, n_chunks=n_chunks, n_pairs=n_pairs),
        grid_spec=grid_spec,
        out_shape=jax.ShapeDtypeStruct((n_pairs + 2 * tm, d), F32),
        compiler_params=pltpu.CompilerParams(
            dimension_semantics=("arbitrary", "arbitrary"),
            vmem_limit_bytes=VMEM_LIMIT_BYTES),
        name="ffn_moe",
    )(tile_expert, tile_rows, src_token, src_token, dst_row, f, gate_row, w1, w3, w2)


def _ple_kernel(h1_ref, y0_ref, y1_ref, p_ref, pg_ref, wg_ref, wp_ref, o_ref):
    h2 = h1_ref[...] + y0_ref[...] + y1_ref[...]
    o_ref[...] = _ple(h2, p_ref[...], pg_ref[...], wg_ref[...], wp_ref[...])


def _ple_combine(h1, y, p, pg, wg, wp, *, tm):
    n, d = h1.shape
    row = lambda i: (i, 0)
    fixed = lambda i: (0, 0)
    return pl.pallas_call(
        _ple_kernel,
        grid=(n // tm,),
        in_specs=[
            pl.BlockSpec((tm, d), row),
            pl.BlockSpec((tm, d), row),
            pl.BlockSpec((tm, d), lambda i: (i + n // tm, 0)),
            pl.BlockSpec((tm, p.shape[1]), row),
            pl.BlockSpec((1, d), fixed),
            pl.BlockSpec(wg.shape, fixed),
            pl.BlockSpec(wp.shape, fixed),
        ],
        out_specs=pl.BlockSpec((tm, d), row),
        out_shape=jax.ShapeDtypeStruct((n, d), F32),
        compiler_params=pltpu.CompilerParams(
            dimension_semantics=("arbitrary",), vmem_limit_bytes=VMEM_LIMIT_BYTES),
        name="ple_combine",
    )(h1, y, y, p, pg, wg, wp)


def kernel(x, p, mix_norm_g, w_in, q_norm_g, k_norm_g, conv_w, conv_b, attn_out_g, conv_out_g,
           w_o, ffn_norm_g, dense_w1, dense_w3, dense_w2, router_w, moe_w1, moe_w3, moe_w2,
           ple_norm_g, ple_gate_w, ple_proj_w):
    bsz, seq_len, d = x.shape
    depth = w_in.shape[0]
    sb = attn_out_g.shape[1]
    n_heads = sb // HEAD_DIM
    n_experts = router_w.shape[2]
    n = bsz * seq_len
    tm = min(ROW_TILE, seq_len)
    t = min(ATTN_TILE, seq_len)
    tm_ffn = min(FFN_ROW_TILE, n)

    head_id = jnp.arange(sb) // HEAD_DIM
    head_mean = jnp.where(head_id[:, None] == head_id[None, :], 1.0 / HEAD_DIM, 0.0).astype(BF16)
    ar = jnp.arange(t)
    tri = (ar[:, None] >= ar[None, :]).astype(BF16)
    q_scale = LOG2E / float(np.sqrt(HEAD_DIM))
    row2 = lambda a: a.reshape(1, -1)

    h = x.reshape(n, d)
    for i in range(depth):
        qg = row2(jnp.tile(q_norm_g[i], n_heads) * q_scale)
        kg = row2(jnp.tile(k_norm_g[i], n_heads))
        q, k, v, cu, b = _mix_in(h, row2(mix_norm_g[i]), w_in[i].astype(BF16), qg, kg,
                                 head_mean, tm=tm, seq_len=seq_len)
        attn = _attention(q, k, v, tri, t=t, pairs=ATTN_PAIRS)
        j = i // 2
        routed = i % 2 == 1
        rw = None
        if routed:
            e_pad = -(-n_experts // SUBLANES) * SUBLANES
            rw = jnp.pad(router_w[j].T, ((0, e_pad - n_experts), (0, 0)))
            rw_hi = rw.astype(BF16)
            rw = jnp.stack([rw_hi, (rw - rw_hi.astype(F32)).astype(BF16)])
        outs = _mix_out(attn, cu, b, h, row2(attn_out_g[i]), row2(conv_out_g[i]), conv_w[i],
                        row2(conv_b[i]), w_o[i].astype(BF16), row2(ffn_norm_g[i]), rw,
                        tm=tm, seq_len=seq_len, n_experts=n_experts if routed else 0)
        pi = p[i].reshape(n, -1)
        ple_w = (row2(ple_norm_g[i]), ple_gate_w[i].astype(BF16), ple_proj_w[i].astype(BF16))
        if routed:
            h1, f, route = outs
            plan = _route_plan(route, n_experts, MOE_ROW_TILE)
            y = _moe_ffn(f, plan, moe_w1[j].astype(BF16), moe_w3[j].astype(BF16),
                         moe_w2[j].astype(BF16), tm=MOE_ROW_TILE, tf=MOE_COL_TILE)
            h = _ple_combine(h1, y, pi, *ple_w, tm=tm)
        else:
            h1, f = outs
            h = _ffn(f, dense_w1[j:j + 1].astype(BF16), dense_w3[j:j + 1].astype(BF16),
                     dense_w2[j:j + 1].astype(BF16), h1, pi, *ple_w, tm=tm_ffn,
                     tf=FFN_COL_TILE)
    return h.reshape(bsz, seq_len, d)
```

```python
import functools

import numpy as np
import jax
import jax.numpy as jnp
from jax import lax
from jax.experimental import pallas as pl
from jax.experimental.pallas import tpu as pltpu

F32 = jnp.float32
BF16 = jnp.bfloat16

EPS = 1e-6
HEAD_DIM = 64
CONV_K = 3
TOP_K = 2
LANES = 128
SUBLANES = 8
HALO_ROWS = 2 * SUBLANES
LOG2E = float(np.log2(np.e))

VMEM_LIMIT_BYTES = 56 * 1024 * 1024

ROW_TILE = 1024
ATTN_TILE = 256
ATTN_PAIRS = 4
SP_CLAMP = 64.0
DONE_LOG2 = 160.0
NO_BLOCK_DECAY = 1e30
FFN_ROW_TILE = 512
FFN_COL_TILE = 1792
MOE_ROW_TILE = 512
MOE_COL_TILE = 1792


def _rms(x, g):
    return x * lax.rsqrt(jnp.mean(x * x, axis=-1, keepdims=True) + EPS) * g


def _dot(a, b):
    return jnp.dot(a, b, preferred_element_type=F32)


def _mix_in_kernel(h_ref, g_ref, w_ref, qg_ref, kg_ref, hm_ref,
                   q_ref, k_ref, v_ref, cu_ref, b_ref, *, sb):
    a = _rms(h_ref[...], g_ref[...]).astype(BF16)
    hm = hm_ref[...]

    def qk_norm(t, g):
        ms = _dot((t * t).astype(BF16), hm)
        return (t * lax.rsqrt(ms + EPS) * g).astype(BF16)

    q_ref[0] = qk_norm(_dot(a, w_ref[:, 0:sb]), qg_ref[...])
    k_ref[0] = qk_norm(_dot(a, w_ref[:, sb:2 * sb]), kg_ref[...])
    v_ref[0] = _dot(a, w_ref[:, 2 * sb:3 * sb]).astype(BF16)
    u = _dot(a, w_ref[:, 3 * sb:4 * sb])
    c = _dot(a, w_ref[:, 4 * sb:5 * sb])
    cu_ref[...] = (c * u).astype(BF16)
    b_ref[...] = _dot(a, w_ref[:, 5 * sb:6 * sb]).astype(BF16)


def _mix_in(h, g, w_in, qg, kg, head_mean, *, tm, seq_len):
    n, d = h.shape
    sb = head_mean.shape[0]
    tiles_per_seq = seq_len // tm
    row = lambda i: (i, 0)
    fixed = lambda i: (0, 0)
    seq_row = lambda i: (i // tiles_per_seq, i % tiles_per_seq, 0)
    out_bf = jax.ShapeDtypeStruct((n // seq_len, seq_len, sb), BF16)
    out_rows = jax.ShapeDtypeStruct((n, sb), BF16)
    return pl.pallas_call(
        functools.partial(_mix_in_kernel, sb=sb),
        grid=(n // tm,),
        in_specs=[
            pl.BlockSpec((tm, d), row),
            pl.BlockSpec((1, d), fixed),
            pl.BlockSpec(w_in.shape, fixed),
            pl.BlockSpec((1, sb), fixed),
            pl.BlockSpec((1, sb), fixed),
            pl.BlockSpec((sb, sb), fixed),
        ],
        out_specs=[pl.BlockSpec((1, tm, sb), seq_row)] * 3 + [pl.BlockSpec((tm, sb), row)] * 2,
        out_shape=[out_bf, out_bf, out_bf, out_rows, out_rows],
        compiler_params=pltpu.CompilerParams(
            dimension_semantics=("arbitrary",), vmem_limit_bytes=VMEM_LIMIT_BYTES),
        name="mix_in",
    )(h, g, w_in, qg, kg, head_mean)


def _attn_kernel(q_ref, k_ref, v_ref, u_ref, o_ref, acc_ref, c_ref, *, t, pairs):
    i = pl.program_id(2)
    heads = 2 * pairs
    q = q_ref[0]
    lane = lax.broadcasted_iota(jnp.int32, (t, LANES), 1)
    lo_half = lane < HEAD_DIM
    q_heads = []
    for p in range(pairs):
        qp = q[:, p * LANES:(p + 1) * LANES]
        q_heads += [jnp.where(lo_half, qp, jnp.zeros_like(qp)),
                    jnp.where(lo_half, jnp.zeros_like(qp), qp)]
    tri = u_ref[...]
    row = lax.broadcasted_iota(jnp.int32, (t, t), 0)
    col = lax.broadcasted_iota(jnp.int32, (t, t), 1)
    causal = col < row

    acc_ref[...] = jnp.zeros_like(acc_ref)
    c_ref[...] = jnp.zeros_like(c_ref)

    def block(j, diagonal, extra_decay=0.0):
        start = pl.multiple_of(j * t, t)
        kj = k_ref[0, pl.ds(start, t), :]
        vj = v_ref[0, pl.ds(start, t), :]
        for hd in range(heads):
            p = hd // 2
            z = lax.dot_general(q_heads[hd], kj[:, p * LANES:(p + 1) * LANES],
                                (((1,), (1,)), ((), ())), preferred_element_type=F32)
            sp = jnp.maximum(z, jnp.log2(1.0 + jnp.exp2(jnp.minimum(z, SP_CLAMP))))
            if diagonal:
                sp = jnp.where(causal, sp, 0.0)
            incl = _dot(sp.astype(BF16), tri)
            w = jnp.exp2(z - incl - (c_ref[hd] + extra_decay))
            if diagonal:
                w = jnp.where(causal, w, 0.0)
            acc_ref[hd] += _dot(w.astype(BF16), vj[:, p * LANES:(p + 1) * LANES])
            c_ref[hd] += incl[:, 0:1]

    def decay_floor():
        m = jnp.min(c_ref[0])
        for hd in range(1, heads):
            m = jnp.minimum(m, jnp.min(c_ref[hd]))
        return m

    block(i, True)
    block(jnp.maximum(i - 1, 0), False, extra_decay=jnp.where(i > 0, 0.0, NO_BLOCK_DECAY))

    def cond(carry):
        j, floor = carry
        return (j >= 0) & (floor < DONE_LOG2)

    def body(carry):
        j, _ = carry
        block(j, False)
        return j - 1, decay_floor()

    lax.while_loop(cond, body, (i - 2, decay_floor()))
    for p in range(pairs):
        o_ref[0, :, p * LANES:(p + 1) * LANES] = jnp.where(
            lo_half, acc_ref[2 * p], acc_ref[2 * p + 1]).astype(o_ref.dtype)


def _attention(q, k, v, tri, *, t, pairs):
    bsz, s, sb = q.shape
    width = LANES * pairs
    blk = lambda b, g, i: (b, i, g)
    seq = lambda b, g, i: (b, 0, g)
    return pl.pallas_call(
        functools.partial(_attn_kernel, t=t, pairs=pairs),
        grid=(bsz, sb // width, s // t),
        in_specs=[
            pl.BlockSpec((1, t, width), blk),
            pl.BlockSpec((1, s, width), seq),
            pl.BlockSpec((1, s, width), seq),
            pl.BlockSpec((t, t), lambda b, g, i: (0, 0)),
        ],
        out_specs=pl.BlockSpec((1, t, width), blk),
        out_shape=jax.ShapeDtypeStruct((bsz, s, sb), BF16),
        scratch_shapes=[pltpu.VMEM((2 * pairs, t, LANES), F32),
                        pltpu.VMEM((2 * pairs, t, 1), F32)],
        compiler_params=pltpu.CompilerParams(
            dimension_semantics=("arbitrary", "arbitrary", "arbitrary"),
            vmem_limit_bytes=VMEM_LIMIT_BYTES),
        name="attention",
    )(q, k, v, tri)


def _mix_out_kernel(*refs, tiles_per_seq, sb, n_experts):
    (attn_ref, cu_ref, halo_ref, b_ref, h_ref, ag_ref, cg_ref, cw_ref, cb_ref,
     wo_ref, fg_ref) = refs[:11]
    if n_experts:
        rw_ref, h1_ref, f_ref, route_ref = refs[11:]
    else:
        h1_ref, f_ref = refs[11:]

    attn_n = _rms(attn_ref[0].astype(F32), ag_ref[...])

    cu = cu_ref[...].astype(F32)
    first = pl.program_id(0) % tiles_per_seq == 0
    halo = jnp.where(first, 0.0, halo_ref[...].astype(F32))
    rowi = lax.broadcasted_iota(jnp.int32, cu.shape, 0)
    prev1 = jnp.where(rowi == 0, halo[HALO_ROWS - 1:HALO_ROWS], pltpu.roll(cu, 1, axis=0))
    prev2 = jnp.where(rowi == 0, halo[HALO_ROWS - 2:HALO_ROWS - 1],
                      jnp.where(rowi == 1, halo[HALO_ROWS - 1:HALO_ROWS],
                                pltpu.roll(cu, 2, axis=0)))
    conv = cw_ref[0:1] * prev2 + cw_ref[1:2] * prev1 + cw_ref[2:3] * cu + cb_ref[...]
    conv_n = _rms(b_ref[...].astype(F32) * conv, cg_ref[...])

    h1 = (h_ref[...] + _dot(attn_n.astype(BF16), wo_ref[0:sb])
          + _dot(conv_n.astype(BF16), wo_ref[sb:]))
    h1_ref[...] = h1
    f = _rms(h1, fg_ref[...])
    f_ref[...] = f.astype(f_ref.dtype)

    if n_experts:
        f_hi = f.astype(BF16)
        f_lo = (f - f_hi.astype(F32)).astype(BF16)
        nt = (((1,), (1,)), ((), ()))
        logits = (lax.dot_general(rw_ref[0], f_hi, nt, preferred_element_type=F32)
                  + (lax.dot_general(rw_ref[0], f_lo, nt, preferred_element_type=F32)
                     + lax.dot_general(rw_ref[1], f_hi, nt, preferred_element_type=F32)))
        e_pad = logits.shape[0]
        eid = lax.broadcasted_iota(jnp.int32, logits.shape, 0).astype(F32)
        neg = jnp.float32(-jnp.inf)
        logits = jnp.where(eid < n_experts, logits, neg)
        m1 = jnp.max(logits, axis=0, keepdims=True)
        i1 = jnp.min(jnp.where(logits == m1, eid, float(e_pad)), axis=0, keepdims=True)
        rest = jnp.where(eid == i1, neg, logits)
        m2 = jnp.max(rest, axis=0, keepdims=True)
        i2 = jnp.min(jnp.where(rest == m2, eid, float(e_pad)), axis=0, keepdims=True)
        e2 = jnp.exp(m2 - m1)
        g1 = 1.0 / (1.0 + e2)
        g2 = e2 * g1
        slot = lax.broadcasted_iota(jnp.int32, route_ref.shape, 0)
        route_ref[...] = jnp.where(
            slot == 0, i1, jnp.where(slot == 1, i2, jnp.where(
                slot == 2, g1, jnp.where(slot == 3, g2, 0.0))))


def _mix_out(attn, cu, b, h, ag, cg, cw, cb, w_o, fg, router_w, *, tm, seq_len, n_experts):
    n, d = h.shape
    sb = attn.shape[2]
    tiles_per_seq = seq_len // tm
    row = lambda i: (i, 0)
    fixed = lambda i: (0, 0)
    halo = lambda i: (jnp.maximum(i * (tm // HALO_ROWS) - 1, 0), 0)
    in_specs = [
        pl.BlockSpec((1, tm, sb), lambda i: (i // tiles_per_seq, i % tiles_per_seq, 0)),
        pl.BlockSpec((tm, sb), row),
        pl.BlockSpec((HALO_ROWS, sb), halo),
        pl.BlockSpec((tm, sb), row),
        pl.BlockSpec((tm, d), row),
        pl.BlockSpec((1, sb), fixed),
        pl.BlockSpec((1, sb), fixed),
        pl.BlockSpec((CONV_K, sb), fixed),
        pl.BlockSpec((1, sb), fixed),
        pl.BlockSpec(w_o.shape, fixed),
        pl.BlockSpec((1, d), fixed),
    ]
    args = [attn, cu, cu, b, h, ag, cg, cw, cb, w_o, fg]
    out_specs = [pl.BlockSpec((tm, d), row), pl.BlockSpec((tm, d), row)]
    out_shape = [jax.ShapeDtypeStruct((n, d), F32),
                 jax.ShapeDtypeStruct((n, d), F32 if n_experts else BF16)]
    if n_experts:
        in_specs.append(pl.BlockSpec(router_w.shape, lambda i: (0, 0, 0)))
        args.append(router_w)
        out_specs.append(pl.BlockSpec((SUBLANES, tm), lambda i: (0, i)))
        out_shape.append(jax.ShapeDtypeStruct((SUBLANES, n), F32))
    return pl.pallas_call(
        functools.partial(_mix_out_kernel, tiles_per_seq=seq_len // tm, sb=sb,
                          n_experts=n_experts),
        grid=(n // tm,),
        in_specs=in_specs,
        out_specs=out_specs,
        out_shape=out_shape,
        compiler_params=pltpu.CompilerParams(
            dimension_semantics=("arbitrary",), vmem_limit_bytes=VMEM_LIMIT_BYTES),
        name="mix_out_moe" if n_experts else "mix_out",
    )(*args)


def _swiglu_step(x, w1_ref, w3_ref, w2_ref):
    a = _dot(x, w1_ref[0])
    hidden = a * jax.nn.sigmoid(a) * _dot(x, w3_ref[0])
    return _dot(hidden.astype(BF16), w2_ref[0])


def _ple(h2, p, pg, wg, wp):
    gate = jax.nn.sigmoid(_dot(_rms(h2, pg).astype(BF16), wg))
    return h2 + gate * _dot(p.astype(BF16), wp)


def _ffn_kernel(f_ref, w1_ref, w3_ref, w2_ref, h1_ref, p_ref, pg_ref, wg_ref, wp_ref,
                o_ref, acc_ref):
    c = pl.program_id(1)

    @pl.when(c == 0)
    def _():
        acc_ref[...] = jnp.zeros_like(acc_ref)

    acc_ref[...] += _swiglu_step(f_ref[...], w1_ref, w3_ref, w2_ref)

    @pl.when(c == pl.num_programs(1) - 1)
    def _():
        o_ref[...] = _ple(h1_ref[...] + acc_ref[...], p_ref[0, 0], pg_ref[...], wg_ref[...],
                          wp_ref[...])


def _layer_rows(p, layer, tm):
    tiles_per_seq = p.shape[2] // tm
    return ((1, 1, tm, p.shape[3]),
            lambda i, *_: (layer, i // tiles_per_seq, i % tiles_per_seq, 0))


def _ffn(f, w1, w3, w2, h1, p, layer, pg, wg, wp, *, tm, tf):
    n, d = h1.shape
    d_ff = w1.shape[2]
    row = lambda i, c: (i, 0)
    fixed = lambda i, c: (0, 0)
    in_specs = [
        pl.BlockSpec((tm, d), row),
        pl.BlockSpec((1, d, tf), lambda i, c: (0, 0, c)),
        pl.BlockSpec((1, d, tf), lambda i, c: (0, 0, c)),
        pl.BlockSpec((1, tf, d), lambda i, c: (0, c, 0)),
        pl.BlockSpec((tm, d), row),
        pl.BlockSpec(*_layer_rows(p, layer, tm)),
        pl.BlockSpec((1, d), fixed),
        pl.BlockSpec(wg.shape, fixed),
        pl.BlockSpec(wp.shape, fixed),
    ]
    return pl.pallas_call(
        _ffn_kernel,
        grid=(n // tm, d_ff // tf),
        in_specs=in_specs,
        out_specs=pl.BlockSpec((tm, d), row),
        out_shape=jax.ShapeDtypeStruct((n, d), F32),
        scratch_shapes=[pltpu.VMEM((tm, d), F32)],
        compiler_params=pltpu.CompilerParams(
            dimension_semantics=("arbitrary", "arbitrary"),
            vmem_limit_bytes=VMEM_LIMIT_BYTES),
        name="ffn_dense",
    )(f, w1, w3, w2, h1, p, pg, wg, wp)


def _route_plan(route, n_experts, tm):
    n = route.shape[1]
    pairs = TOP_K * n
    n_rows = pairs + n_experts * tm
    eid = route[0:TOP_K].T.astype(jnp.int32).reshape(pairs)
    gates = route[TOP_K:2 * TOP_K].T.reshape(pairs)
    pair_bits = int(np.ceil(np.log2(pairs)))
    keys = jnp.sort(eid * (1 << pair_bits) + jnp.arange(pairs, dtype=jnp.int32))
    order = keys & ((1 << pair_bits) - 1)
    bounds = jnp.arange(n_experts + 1, dtype=jnp.int32) << pair_bits
    bounds = bounds.at[n_experts].set(jnp.iinfo(jnp.int32).max)
    first = jnp.sum(keys[None, :] < bounds[:, None], axis=1).astype(jnp.int32)
    counts = first[1:] - first[:-1]
    padded = (counts + tm - 1) // tm * tm
    ends = jnp.cumsum(padded)
    n_tiles = n_rows // tm
    tile_start = jnp.arange(n_tiles, dtype=jnp.int32) * tm
    tile_expert = jnp.sum(tile_start[:, None] >= ends[None, :], axis=1)
    tile_expert = jnp.minimum(tile_expert, n_experts - 1).astype(jnp.int32)
    in_group = tile_start - (ends - padded)[tile_expert]
    tile_rows = jnp.clip(counts[tile_expert] - in_group, 0, tm)
    lane = jnp.arange(tm, dtype=jnp.int32)[None, :]
    real = lane < tile_rows[:, None]
    safe = order[jnp.clip((first[tile_expert] + in_group)[:, None] + lane, 0, pairs - 1)]
    src_token = jnp.where(real, safe // TOP_K, 0)
    gate_row = jnp.where(real, gates[safe], 0.0)
    spare = pairs + (jnp.arange(n_tiles, dtype=jnp.int32) % 2 * tm)[:, None] + lane
    dst_row = jnp.where(real, (safe % TOP_K) * n + safe // TOP_K, spare)
    src_token = jnp.concatenate([src_token, jnp.zeros((1, tm), jnp.int32)])
    dst_row = jnp.concatenate([pairs + tm + lane, dst_row])
    return (src_token.reshape(n_tiles + 1, 1, tm), dst_row.reshape(n_tiles + 1, 1, tm),
            jnp.broadcast_to(gate_row[:, None, :], (n_tiles, SUBLANES, tm)), tile_expert,
            tile_rows)


def _for_each_row(count, fn, unroll=8):
    if isinstance(count, int):
        full = count // unroll
    else:
        full = lax.shift_right_logical(count, jnp.int32(int(np.log2(unroll))))

    def chunk(i, carry):
        for u in range(unroll):
            fn(i * unroll + u)
        return carry

    def single(r, carry):
        fn(r)
        return carry

    lax.fori_loop(0, full, chunk, 0)
    lax.fori_loop(full * unroll, count, single, 0)


def _moe_ffn_kernel(expert_ref, count_ref, src0_ref, src_ref, dst_ref, f_hbm, gate_ref,
                    w1_ref, w3_ref, w2_ref, y_hbm, rows_ref, x_ref, acc_ref, stage_ref,
                    gather_sem, scatter_sem, *, tm, n_chunks, n_pairs):
    j = pl.program_id(0)
    c = pl.program_id(1)
    last_tile = pl.num_programs(0) - 1
    last_chunk = n_chunks - 1
    per_step = tm // n_chunks
    active = count_ref[j] > 0
    cur = j % 2
    other = 1 - cur

    def tile_gather(slot):
        return pltpu.make_async_copy(f_hbm.at[pl.ds(0, tm), :], rows_ref.at[slot],
                                     gather_sem.at[slot])

    def tile_scatter(slot):
        return pltpu.make_async_copy(stage_ref.at[slot], y_hbm.at[pl.ds(0, tm), :],
                                     scatter_sem.at[slot])

    @pl.when((j == 0) & (c == 0))
    def _():
        stage_ref[...] = jnp.zeros_like(stage_ref)
        for slot in range(2):
            spare = pltpu.make_async_copy(stage_ref.at[slot],
                                          y_hbm.at[pl.ds(n_pairs + slot * tm, tm), :],
                                          scatter_sem.at[slot])
            spare.start()
            spare.wait()

        def gather_first(r):
            pltpu.make_async_copy(f_hbm.at[pl.ds(src0_ref[0, 0, r], 1), :],
                                  rows_ref.at[0, pl.ds(r, 1), :], gather_sem.at[0]).start()
        _for_each_row(tm, gather_first)

    @pl.when(c == 0)
    def _():
        tile_gather(cur).wait()

        @pl.when(j >= 1)
        def _():
            tile_scatter(cur).wait()

        x_ref[...] = rows_ref[cur].astype(BF16)
        acc_ref[...] = jnp.zeros_like(acc_ref)

    def start_row_copies():
        base = pl.multiple_of(c * per_step, per_step)
        for r in range(per_step):
            pltpu.make_async_copy(f_hbm.at[pl.ds(src_ref[0, 0, base + r], 1), :],
                                  rows_ref.at[other, pl.ds(base + r, 1), :],
                                  gather_sem.at[other]).start()
            pltpu.make_async_copy(stage_ref.at[other, pl.ds(base + r, 1), :],
                                  y_hbm.at[pl.ds(dst_ref[0, 0, base + r], 1), :],
                                  scatter_sem.at[other]).start()

    @pl.when(active)
    def _():
        start_row_copies()
        acc_ref[...] += _swiglu_step(x_ref[...], w1_ref, w3_ref, w2_ref)

    @pl.when(jnp.logical_not(active))
    def _():
        start_row_copies()

    @pl.when(active & (c == last_chunk))
    def _():
        gate = jnp.transpose(gate_ref[0])[:, 0:1]
        stage_ref[cur] = acc_ref[...] * gate

    @pl.when((j == last_tile) & (c == last_chunk))
    def _():
        tile_gather(other).wait()
        tile_scatter(other).wait()


def _moe_ffn(f, plan, w1, w3, w2, *, tm, tf):
    src_token, dst_row, gate_row, tile_expert, tile_rows = plan
    n, d = f.shape
    n_exp, _, d_ff = w1.shape
    n_tiles = tile_rows.shape[0]
    n_chunks = d_ff // tf
    n_pairs = TOP_K * n
    assert n_pairs % tm == 0 and tm % n_chunks == 0 and n_exp < tm
    chunk = lambda j, c, count: jnp.where(count[j] > 0, c, n_chunks - 1)
    grid_spec = pltpu.PrefetchScalarGridSpec(
        num_scalar_prefetch=2,
        grid=(n_tiles, n_chunks),
        in_specs=[
            pl.BlockSpec((1, 1, tm), lambda j, c, ex, count: (0, 0, 0), memory_space=pltpu.SMEM),
            pl.BlockSpec((1, 1, tm), lambda j, c, ex, count: (j + 1, 0, 0),
                         memory_space=pltpu.SMEM),
            pl.BlockSpec((1, 1, tm), lambda j, c, ex, count: (j, 0, 0), memory_space=pltpu.SMEM),
            pl.BlockSpec(memory_space=pl.ANY),
            pl.BlockSpec((1, SUBLANES, tm), lambda j, c, ex, count: (j, 0, 0)),
            pl.BlockSpec((1, d, tf), lambda j, c, ex, count: (ex[j], 0, chunk(j, c, count))),
            pl.BlockSpec((1, d, tf), lambda j, c, ex, count: (ex[j], 0, chunk(j, c, count))),
            pl.BlockSpec((1, tf, d), lambda j, c, ex, count: (ex[j], chunk(j, c, count), 0)),
        ],
        out_specs=pl.BlockSpec(memory_space=pl.ANY),
        scratch_shapes=[pltpu.VMEM((2, tm, d), F32), pltpu.VMEM((tm, d), BF16),
                        pltpu.VMEM((tm, d), F32), pltpu.VMEM((2, tm, d), F32),
                        pltpu.SemaphoreType.DMA((2,)), pltpu.SemaphoreType.DMA((2,))],
    )
    return pl.pallas_call(
        functools.partial(_moe_ffn_kernel, tm=tm, n_chunks=n_chunks, n_pairs=n_pairs),
        grid_spec=grid_spec,
        out_shape=jax.ShapeDtypeStruct((n_pairs + 2 * tm, d), F32),
        compiler_params=pltpu.CompilerParams(
            dimension_semantics=("arbitrary", "arbitrary"),
            vmem_limit_bytes=VMEM_LIMIT_BYTES),
        name="ffn_moe",
    )(tile_expert, tile_rows, src_token, src_token, dst_row, f, gate_row, w1, w3, w2)


def _ple_kernel(h1_ref, y0_ref, y1_ref, p_ref, pg_ref, wg_ref, wp_ref, o_ref):
    h2 = h1_ref[...] + y0_ref[...] + y1_ref[...]
    o_ref[...] = _ple(h2, p_ref[0, 0], pg_ref[...], wg_ref[...], wp_ref[...])


def _ple_combine(h1, y, p, layer, pg, wg, wp, *, tm):
    n, d = h1.shape
    row = lambda i: (i, 0)
    fixed = lambda i: (0, 0)
    return pl.pallas_call(
        _ple_kernel,
        grid=(n // tm,),
        in_specs=[
            pl.BlockSpec((tm, d), row),
            pl.BlockSpec((tm, d), row),
            pl.BlockSpec((tm, d), lambda i: (i + n // tm, 0)),
            pl.BlockSpec(*_layer_rows(p, layer, tm)),
            pl.BlockSpec((1, d), fixed),
            pl.BlockSpec(wg.shape, fixed),
            pl.BlockSpec(wp.shape, fixed),
        ],
        out_specs=pl.BlockSpec((tm, d), row),
        out_shape=jax.ShapeDtypeStruct((n, d), F32),
        compiler_params=pltpu.CompilerParams(
            dimension_semantics=("arbitrary",), vmem_limit_bytes=VMEM_LIMIT_BYTES),
        name="ple_combine",
    )(h1, y, y, p, pg, wg, wp)


def kernel(x, p, mix_norm_g, w_in, q_norm_g, k_norm_g, conv_w, conv_b, attn_out_g, conv_out_g,
           w_o, ffn_norm_g, dense_w1, dense_w3, dense_w2, router_w, moe_w1, moe_w3, moe_w2,
           ple_norm_g, ple_gate_w, ple_proj_w):
    bsz, seq_len, d = x.shape
    depth = w_in.shape[0]
    sb = attn_out_g.shape[1]
    n_heads = sb // HEAD_DIM
    n_experts = router_w.shape[2]
    n = bsz * seq_len
    tm = min(ROW_TILE, seq_len)
    t = min(ATTN_TILE, seq_len)
    tm_ffn = min(FFN_ROW_TILE, seq_len)

    head_id = jnp.arange(sb) // HEAD_DIM
    head_mean = jnp.where(head_id[:, None] == head_id[None, :], 1.0 / HEAD_DIM, 0.0).astype(BF16)
    ar = jnp.arange(t)
    tri = (ar[:, None] >= ar[None, :]).astype(BF16)
    q_scale = LOG2E / float(np.sqrt(HEAD_DIM))
    row2 = lambda a: a.reshape(1, -1)

    h = x.reshape(n, d)
    for i in range(depth):
        qg = row2(jnp.tile(q_norm_g[i], n_heads) * q_scale)
        kg = row2(jnp.tile(k_norm_g[i], n_heads))
        q, k, v, cu, b = _mix_in(h, row2(mix_norm_g[i]), w_in[i].astype(BF16), qg, kg,
                                 head_mean, tm=tm, seq_len=seq_len)
        attn = _attention(q, k, v, tri, t=t, pairs=ATTN_PAIRS)
        j = i // 2
        routed = i % 2 == 1
        rw = None
        if routed:
            e_pad = -(-n_experts // SUBLANES) * SUBLANES
            rw = jnp.pad(router_w[j].T, ((0, e_pad - n_experts), (0, 0)))
            rw_hi = rw.astype(BF16)
            rw = jnp.stack([rw_hi, (rw - rw_hi.astype(F32)).astype(BF16)])
        outs = _mix_out(attn, cu, b, h, row2(attn_out_g[i]), row2(conv_out_g[i]), conv_w[i],
                        row2(conv_b[i]), w_o[i].astype(BF16), row2(ffn_norm_g[i]), rw,
                        tm=tm, seq_len=seq_len, n_experts=n_experts if routed else 0)
        ple_w = (row2(ple_norm_g[i]), ple_gate_w[i].astype(BF16), ple_proj_w[i].astype(BF16))
        if routed:
            h1, f, route = outs
            plan = _route_plan(route, n_experts, MOE_ROW_TILE)
            y = _moe_ffn(f, plan, moe_w1[j].astype(BF16), moe_w3[j].astype(BF16),
                         moe_w2[j].astype(BF16), tm=MOE_ROW_TILE, tf=MOE_COL_TILE)
            h = _ple_combine(h1, y, p, i, *ple_w, tm=tm)
        else:
            h1, f = outs
            h = _ffn(f, dense_w1[j:j + 1].astype(BF16), dense_w3[j:j + 1].astype(BF16),
                     dense_w2[j:j + 1].astype(BF16), h1, p, i, *ple_w, tm=tm_ffn,
                     tf=FFN_COL_TILE)
    return h.reshape(bsz, seq_len, d)
```

```python
import functools

import numpy as np
import jax
import jax.numpy as jnp
from jax import lax
from jax.experimental import pallas as pl
from jax.experimental.pallas import tpu as pltpu

F32 = jnp.float32
BF16 = jnp.bfloat16

EPS = 1e-6
HEAD_DIM = 64
CONV_K = 3
TOP_K = 2
LANES = 128
SUBLANES = 8
HALO_ROWS = 2 * SUBLANES
LOG2E = float(np.log2(np.e))

VMEM_LIMIT_BYTES = 56 * 1024 * 1024

ROW_TILE = 1024
ATTN_TILE = 256
ATTN_PAIRS = 4
SP_CLAMP = 64.0
DONE_LOG2 = 160.0
NO_BLOCK_DECAY = 1e30
FFN_ROW_TILE = 512
FFN_COL_TILE = 1792
MOE_ROW_TILE = 512
MOE_COL_TILE = 1792


def _rms(x, g):
    return x * lax.rsqrt(jnp.mean(x * x, axis=-1, keepdims=True) + EPS) * g


def _dot(a, b):
    return jnp.dot(a, b, preferred_element_type=F32)


def _mix_in_kernel(h_ref, g_ref, w_ref, qg_ref, kg_ref, hm_ref,
                   q_ref, k_ref, v_ref, cu_ref, b_ref, *, sb):
    a = _rms(h_ref[...], g_ref[...]).astype(BF16)
    hm = hm_ref[...]

    def qk_norm(t, g):
        ms = _dot((t * t).astype(BF16), hm)
        return (t * lax.rsqrt(ms + EPS) * g).astype(BF16)

    q_ref[0] = qk_norm(_dot(a, w_ref[:, 0:sb]), qg_ref[...])
    k_ref[0] = qk_norm(_dot(a, w_ref[:, sb:2 * sb]), kg_ref[...])
    v_ref[0] = _dot(a, w_ref[:, 2 * sb:3 * sb]).astype(BF16)
    u = _dot(a, w_ref[:, 3 * sb:4 * sb])
    c = _dot(a, w_ref[:, 4 * sb:5 * sb])
    cu_ref[...] = (c * u).astype(BF16)
    b_ref[...] = _dot(a, w_ref[:, 5 * sb:6 * sb]).astype(BF16)


def _mix_in(h, g, w_in, qg, kg, head_mean, *, tm, seq_len):
    n, d = h.shape
    sb = head_mean.shape[0]
    tiles_per_seq = seq_len // tm
    row = lambda i: (i, 0)
    fixed = lambda i: (0, 0)
    seq_row = lambda i: (i // tiles_per_seq, i % tiles_per_seq, 0)
    out_bf = jax.ShapeDtypeStruct((n // seq_len, seq_len, sb), BF16)
    out_rows = jax.ShapeDtypeStruct((n, sb), BF16)
    return pl.pallas_call(
        functools.partial(_mix_in_kernel, sb=sb),
        grid=(n // tm,),
        in_specs=[
            pl.BlockSpec((tm, d), row),
            pl.BlockSpec((1, d), fixed),
            pl.BlockSpec(w_in.shape, fixed),
            pl.BlockSpec((1, sb), fixed),
            pl.BlockSpec((1, sb), fixed),
            pl.BlockSpec((sb, sb), fixed),
        ],
        out_specs=[pl.BlockSpec((1, tm, sb), seq_row)] * 3 + [pl.BlockSpec((tm, sb), row)] * 2,
        out_shape=[out_bf, out_bf, out_bf, out_rows, out_rows],
        compiler_params=pltpu.CompilerParams(
            dimension_semantics=("arbitrary",), vmem_limit_bytes=VMEM_LIMIT_BYTES),
        name="mix_in",
    )(h, g, w_in, qg, kg, head_mean)


def _attn_kernel(q_ref, k_ref, v_ref, u_ref, o_ref, acc_ref, c_ref, *, t, pairs):
    i = pl.program_id(2)
    heads = 2 * pairs
    q = q_ref[0]
    lane = lax.broadcasted_iota(jnp.int32, (t, LANES), 1)
    lo_half = lane < HEAD_DIM
    q_heads = []
    for p in range(pairs):
        qp = q[:, p * LANES:(p + 1) * LANES]
        q_heads += [jnp.where(lo_half, qp, jnp.zeros_like(qp)),
                    jnp.where(lo_half, jnp.zeros_like(qp), qp)]
    tri = u_ref[...]
    row = lax.broadcasted_iota(jnp.int32, (t, t), 0)
    col = lax.broadcasted_iota(jnp.int32, (t, t), 1)
    causal = col < row

    acc_ref[...] = jnp.zeros_like(acc_ref)
    c_ref[...] = jnp.zeros_like(c_ref)

    def block(j, diagonal, extra_decay=0.0):
        start = pl.multiple_of(j * t, t)
        kj = k_ref[0, pl.ds(start, t), :]
        vj = v_ref[0, pl.ds(start, t), :]
        for hd in range(heads):
            p = hd // 2
            z = lax.dot_general(q_heads[hd], kj[:, p * LANES:(p + 1) * LANES],
                                (((1,), (1,)), ((), ())), preferred_element_type=F32)
            sp = jnp.maximum(z, jnp.log2(1.0 + jnp.exp2(jnp.minimum(z, SP_CLAMP))))
            if diagonal:
                sp = jnp.where(causal, sp, 0.0)
            incl = _dot(sp.astype(BF16), tri)
            w = jnp.exp2(z - incl - (c_ref[hd] + extra_decay))
            if diagonal:
                w = jnp.where(causal, w, 0.0)
            acc_ref[hd] += _dot(w.astype(BF16), vj[:, p * LANES:(p + 1) * LANES])
            c_ref[hd] += incl[:, 0:1]

    def decay_floor():
        m = jnp.min(c_ref[0])
        for hd in range(1, heads):
            m = jnp.minimum(m, jnp.min(c_ref[hd]))
        return m

    block(i, True)
    block(jnp.maximum(i - 1, 0), False, extra_decay=jnp.where(i > 0, 0.0, NO_BLOCK_DECAY))

    def cond(carry):
        j, floor = carry
        return (j >= 0) & (floor < DONE_LOG2)

    def body(carry):
        j, _ = carry
        block(j, False)
        return j - 1, decay_floor()

    lax.while_loop(cond, body, (i - 2, decay_floor()))
    for p in range(pairs):
        o_ref[0, :, p * LANES:(p + 1) * LANES] = jnp.where(
            lo_half, acc_ref[2 * p], acc_ref[2 * p + 1]).astype(o_ref.dtype)


def _attention(q, k, v, tri, *, t, pairs):
    bsz, s, sb = q.shape
    width = LANES * pairs
    blk = lambda b, g, i: (b, i, g)
    seq = lambda b, g, i: (b, 0, g)
    return pl.pallas_call(
        functools.partial(_attn_kernel, t=t, pairs=pairs),
        grid=(bsz, sb // width, s // t),
        in_specs=[
            pl.BlockSpec((1, t, width), blk),
            pl.BlockSpec((1, s, width), seq),
            pl.BlockSpec((1, s, width), seq),
            pl.BlockSpec((t, t), lambda b, g, i: (0, 0)),
        ],
        out_specs=pl.BlockSpec((1, t, width), blk),
        out_shape=jax.ShapeDtypeStruct((bsz, s, sb), BF16),
        scratch_shapes=[pltpu.VMEM((2 * pairs, t, LANES), F32),
                        pltpu.VMEM((2 * pairs, t, 1), F32)],
        compiler_params=pltpu.CompilerParams(
            dimension_semantics=("arbitrary", "arbitrary", "arbitrary"),
            vmem_limit_bytes=VMEM_LIMIT_BYTES),
        name="attention",
    )(q, k, v, tri)


def _mix_out_kernel(*refs, tiles_per_seq, sb, n_experts):
    (attn_ref, cu_ref, halo_ref, b_ref, h_ref, ag_ref, cg_ref, cw_ref, cb_ref,
     wo_ref, fg_ref) = refs[:11]
    if n_experts:
        rw_ref, h1_ref, f_ref, route_ref = refs[11:]
    else:
        h1_ref, f_ref = refs[11:]

    attn_n = _rms(attn_ref[0].astype(F32), ag_ref[...])

    cu = cu_ref[...].astype(F32)
    first = pl.program_id(0) % tiles_per_seq == 0
    halo = jnp.where(first, 0.0, halo_ref[...].astype(F32))
    rowi = lax.broadcasted_iota(jnp.int32, cu.shape, 0)
    prev1 = jnp.where(rowi == 0, halo[HALO_ROWS - 1:HALO_ROWS], pltpu.roll(cu, 1, axis=0))
    prev2 = jnp.where(rowi == 0, halo[HALO_ROWS - 2:HALO_ROWS - 1],
                      jnp.where(rowi == 1, halo[HALO_ROWS - 1:HALO_ROWS],
                                pltpu.roll(cu, 2, axis=0)))
    conv = cw_ref[0:1] * prev2 + cw_ref[1:2] * prev1 + cw_ref[2:3] * cu + cb_ref[...]
    conv_n = _rms(b_ref[...].astype(F32) * conv, cg_ref[...])

    h1 = (h_ref[...] + _dot(attn_n.astype(BF16), wo_ref[0:sb])
          + _dot(conv_n.astype(BF16), wo_ref[sb:]))
    h1_ref[...] = h1
    f = _rms(h1, fg_ref[...])
    f_ref[...] = f.astype(f_ref.dtype)

    if n_experts:
        f_hi = f.astype(BF16)
        f_lo = (f - f_hi.astype(F32)).astype(BF16)
        nt = (((1,), (1,)), ((), ()))
        logits = (lax.dot_general(rw_ref[0], f_hi, nt, preferred_element_type=F32)
                  + (lax.dot_general(rw_ref[0], f_lo, nt, preferred_element_type=F32)
                     + lax.dot_general(rw_ref[1], f_hi, nt, preferred_element_type=F32)))
        e_pad = logits.shape[0]
        eid = lax.broadcasted_iota(jnp.int32, logits.shape, 0).astype(F32)
        neg = jnp.float32(-jnp.inf)
        logits = jnp.where(eid < n_experts, logits, neg)
        m1 = jnp.max(logits, axis=0, keepdims=True)
        i1 = jnp.min(jnp.where(logits == m1, eid, float(e_pad)), axis=0, keepdims=True)
        rest = jnp.where(eid == i1, neg, logits)
        m2 = jnp.max(rest, axis=0, keepdims=True)
        i2 = jnp.min(jnp.where(rest == m2, eid, float(e_pad)), axis=0, keepdims=True)
        e2 = jnp.exp(m2 - m1)
        g1 = 1.0 / (1.0 + e2)
        g2 = e2 * g1
        slot = lax.broadcasted_iota(jnp.int32, route_ref.shape, 0)
        route_ref[...] = jnp.where(
            slot == 0, i1, jnp.where(slot == 1, i2, jnp.where(
                slot == 2, g1, jnp.where(slot == 3, g2, 0.0))))


def _mix_out(attn, cu, b, h, ag, cg, cw, cb, w_o, fg, router_w, *, tm, seq_len, n_experts):
    n, d = h.shape
    sb = attn.shape[2]
    tiles_per_seq = seq_len // tm
    row = lambda i: (i, 0)
    fixed = lambda i: (0, 0)
    halo = lambda i: (jnp.maximum(i * (tm // HALO_ROWS) - 1, 0), 0)
    in_specs = [
        pl.BlockSpec((1, tm, sb), lambda i: (i // tiles_per_seq, i % tiles_per_seq, 0)),
        pl.BlockSpec((tm, sb), row),
        pl.BlockSpec((HALO_ROWS, sb), halo),
        pl.BlockSpec((tm, sb), row),
        pl.BlockSpec((tm, d), row),
        pl.BlockSpec((1, sb), fixed),
        pl.BlockSpec((1, sb), fixed),
        pl.BlockSpec((CONV_K, sb), fixed),
        pl.BlockSpec((1, sb), fixed),
        pl.BlockSpec(w_o.shape, fixed),
        pl.BlockSpec((1, d), fixed),
    ]
    args = [attn, cu, cu, b, h, ag, cg, cw, cb, w_o, fg]
    out_specs = [pl.BlockSpec((tm, d), row), pl.BlockSpec((tm, d), row)]
    out_shape = [jax.ShapeDtypeStruct((n, d), F32),
                 jax.ShapeDtypeStruct((n, d), F32 if n_experts else BF16)]
    if n_experts:
        in_specs.append(pl.BlockSpec(router_w.shape, lambda i: (0, 0, 0)))
        args.append(router_w)
        out_specs.append(pl.BlockSpec((SUBLANES, tm), lambda i: (0, i)))
        out_shape.append(jax.ShapeDtypeStruct((SUBLANES, n), F32))
    return pl.pallas_call(
        functools.partial(_mix_out_kernel, tiles_per_seq=seq_len // tm, sb=sb,
                          n_experts=n_experts),
        grid=(n // tm,),
        in_specs=in_specs,
        out_specs=out_specs,
        out_shape=out_shape,
        compiler_params=pltpu.CompilerParams(
            dimension_semantics=("arbitrary",), vmem_limit_bytes=VMEM_LIMIT_BYTES),
        name="mix_out_moe" if n_experts else "mix_out",
    )(*args)


def _swiglu_step(x, w1_ref, w3_ref, w2_ref):
    a = _dot(x, w1_ref[0])
    hidden = a * jax.nn.sigmoid(a) * _dot(x, w3_ref[0])
    return _dot(hidden.astype(BF16), w2_ref[0])


def _ple(h2, p, pg, wg, wp):
    gate = jax.nn.sigmoid(_dot(_rms(h2, pg).astype(BF16), wg))
    return h2 + gate * _dot(p.astype(BF16), wp)


def _ffn_kernel(f_ref, w1_ref, w3_ref, w2_ref, h1_ref, p_ref, pg_ref, wg_ref, wp_ref,
                o_ref, acc_ref):
    c = pl.program_id(1)

    @pl.when(c == 0)
    def _():
        acc_ref[...] = jnp.zeros_like(acc_ref)

    acc_ref[...] += _swiglu_step(f_ref[...], w1_ref, w3_ref, w2_ref)

    @pl.when(c == pl.num_programs(1) - 1)
    def _():
        o_ref[...] = _ple(h1_ref[...] + acc_ref[...], p_ref[0, 0], pg_ref[...], wg_ref[...],
                          wp_ref[...])


def _layer_rows(p, layer, tm):
    tiles_per_seq = p.shape[2] // tm
    return ((1, 1, tm, p.shape[3]),
            lambda i, *_: (layer, i // tiles_per_seq, i % tiles_per_seq, 0))


def _ffn(f, w1, w3, w2, h1, p, layer, pg, wg, wp, *, tm, tf):
    n, d = h1.shape
    d_ff = w1.shape[2]
    row = lambda i, c: (i, 0)
    fixed = lambda i, c: (0, 0)
    in_specs = [
        pl.BlockSpec((tm, d), row),
        pl.BlockSpec((1, d, tf), lambda i, c: (0, 0, c)),
        pl.BlockSpec((1, d, tf), lambda i, c: (0, 0, c)),
        pl.BlockSpec((1, tf, d), lambda i, c: (0, c, 0)),
        pl.BlockSpec((tm, d), row),
        pl.BlockSpec(*_layer_rows(p, layer, tm)),
        pl.BlockSpec((1, d), fixed),
        pl.BlockSpec(wg.shape, fixed),
        pl.BlockSpec(wp.shape, fixed),
    ]
    return pl.pallas_call(
        _ffn_kernel,
        grid=(n // tm, d_ff // tf),
        in_specs=in_specs,
        out_specs=pl.BlockSpec((tm, d), row),
        out_shape=jax.ShapeDtypeStruct((n, d), F32),
        scratch_shapes=[pltpu.VMEM((tm, d), F32)],
        compiler_params=pltpu.CompilerParams(
            dimension_semantics=("arbitrary", "arbitrary"),
            vmem_limit_bytes=VMEM_LIMIT_BYTES),
        name="ffn_dense",
    )(f, w1, w3, w2, h1, p, pg, wg, wp)


def _route_plan(route, n_experts, tm):
    n = route.shape[1]
    pairs = TOP_K * n
    n_rows = pairs + n_experts * tm
    eid = route[0:TOP_K].astype(jnp.int32).reshape(pairs)
    gates = route[TOP_K:2 * TOP_K].reshape(pairs)
    pair_bits = int(np.ceil(np.log2(pairs)))
    keys = jnp.sort(eid * (1 << pair_bits) + jnp.arange(pairs, dtype=jnp.int32))
    order = keys & ((1 << pair_bits) - 1)
    bounds = jnp.arange(n_experts + 1, dtype=jnp.int32) << pair_bits
    bounds = bounds.at[n_experts].set(jnp.iinfo(jnp.int32).max)
    first = jnp.sum(keys[None, :] < bounds[:, None], axis=1).astype(jnp.int32)
    counts = first[1:] - first[:-1]
    padded = (counts + tm - 1) // tm * tm
    ends = jnp.cumsum(padded)
    n_tiles = n_rows // tm
    tile_start = jnp.arange(n_tiles, dtype=jnp.int32) * tm
    tile_expert = jnp.sum(tile_start[:, None] >= ends[None, :], axis=1)
    tile_expert = jnp.minimum(tile_expert, n_experts - 1).astype(jnp.int32)
    in_group = tile_start - (ends - padded)[tile_expert]
    tile_rows = jnp.clip(counts[tile_expert] - in_group, 0, tm)
    lane = jnp.arange(tm, dtype=jnp.int32)[None, :]
    real = lane < tile_rows[:, None]
    safe = order[jnp.clip((first[tile_expert] + in_group)[:, None] + lane, 0, pairs - 1)]
    src_token = jnp.where(real, safe % n, 0)
    gate_row = jnp.where(real, gates[safe], 0.0)
    spare = pairs + (jnp.arange(n_tiles, dtype=jnp.int32) % 2 * tm)[:, None] + lane
    dst_row = jnp.where(real, safe, spare)
    src_token = jnp.concatenate([src_token, jnp.zeros((1, tm), jnp.int32)])
    dst_row = jnp.concatenate([pairs + tm + lane, dst_row])
    return (src_token.reshape(n_tiles + 1, 1, tm), dst_row.reshape(n_tiles + 1, 1, tm),
            jnp.broadcast_to(gate_row[:, None, :], (n_tiles, SUBLANES, tm)), tile_expert,
            tile_rows)


def _for_each_row(count, fn, unroll=8):
    if isinstance(count, int):
        full = count // unroll
    else:
        full = lax.shift_right_logical(count, jnp.int32(int(np.log2(unroll))))

    def chunk(i, carry):
        for u in range(unroll):
            fn(i * unroll + u)
        return carry

    def single(r, carry):
        fn(r)
        return carry

    lax.fori_loop(0, full, chunk, 0)
    lax.fori_loop(full * unroll, count, single, 0)


def _moe_ffn_kernel(expert_ref, count_ref, src0_ref, src_ref, dst_ref, f_hbm, gate_ref,
                    w1_ref, w3_ref, w2_ref, y_hbm, rows_ref, x_ref, acc_ref, stage_ref,
                    gather_sem, scatter_sem, *, tm, n_chunks, n_pairs):
    j = pl.program_id(0)
    c = pl.program_id(1)
    last_tile = pl.num_programs(0) - 1
    last_chunk = n_chunks - 1
    per_step = tm // n_chunks
    active = count_ref[j] > 0
    cur = j % 2
    other = 1 - cur

    def tile_gather(slot):
        return pltpu.make_async_copy(f_hbm.at[pl.ds(0, tm), :], rows_ref.at[slot],
                                     gather_sem.at[slot])

    def tile_scatter(slot):
        return pltpu.make_async_copy(stage_ref.at[slot], y_hbm.at[pl.ds(0, tm), :],
                                     scatter_sem.at[slot])

    @pl.when((j == 0) & (c == 0))
    def _():
        stage_ref[...] = jnp.zeros_like(stage_ref)
        for slot in range(2):
            spare = pltpu.make_async_copy(stage_ref.at[slot],
                                          y_hbm.at[pl.ds(n_pairs + slot * tm, tm), :],
                                          scatter_sem.at[slot])
            spare.start()
            spare.wait()

        def gather_first(r):
            pltpu.make_async_copy(f_hbm.at[pl.ds(src0_ref[0, 0, r], 1), :],
                                  rows_ref.at[0, pl.ds(r, 1), :], gather_sem.at[0]).start()
        _for_each_row(tm, gather_first)

    @pl.when(c == 0)
    def _():
        tile_gather(cur).wait()

        @pl.when(j >= 1)
        def _():
            tile_scatter(cur).wait()

        x_ref[...] = rows_ref[cur].astype(BF16)
        acc_ref[...] = jnp.zeros_like(acc_ref)

    def start_row_copies():
        base = pl.multiple_of(c * per_step, per_step)
        for r in range(per_step):
            pltpu.make_async_copy(f_hbm.at[pl.ds(src_ref[0, 0, base + r], 1), :],
                                  rows_ref.at[other, pl.ds(base + r, 1), :],
                                  gather_sem.at[other]).start()
            pltpu.make_async_copy(stage_ref.at[other, pl.ds(base + r, 1), :],
                                  y_hbm.at[pl.ds(dst_ref[0, 0, base + r], 1), :],
                                  scatter_sem.at[other]).start()

    @pl.when(active)
    def _():
        start_row_copies()
        acc_ref[...] += _swiglu_step(x_ref[...], w1_ref, w3_ref, w2_ref)

    @pl.when(jnp.logical_not(active))
    def _():
        start_row_copies()

    @pl.when(active & (c == last_chunk))
    def _():
        gate = jnp.transpose(gate_ref[0])[:, 0:1]
        stage_ref[cur] = acc_ref[...] * gate

    @pl.when((j == last_tile) & (c == last_chunk))
    def _():
        tile_gather(other).wait()
        tile_scatter(other).wait()


def _moe_ffn(f, plan, w1, w3, w2, *, tm, tf):
    src_token, dst_row, gate_row, tile_expert, tile_rows = plan
    n, d = f.shape
    n_exp, _, d_ff = w1.shape
    n_tiles = tile_rows.shape[0]
    n_chunks = d_ff // tf
    n_pairs = TOP_K * n
    assert n_pairs % tm == 0 and tm % n_chunks == 0 and n_exp < tm
    chunk = lambda j, c, count: jnp.where(count[j] > 0, c, n_chunks - 1)
    grid_spec = pltpu.PrefetchScalarGridSpec(
        num_scalar_prefetch=2,
        grid=(n_tiles, n_chunks),
        in_specs=[
            pl.BlockSpec((1, 1, tm), lambda j, c, ex, count: (0, 0, 0), memory_space=pltpu.SMEM),
            pl.BlockSpec((1, 1, tm), lambda j, c, ex, count: (j + 1, 0, 0),
                         memory_space=pltpu.SMEM),
            pl.BlockSpec((1, 1, tm), lambda j, c, ex, count: (j, 0, 0), memory_space=pltpu.SMEM),
            pl.BlockSpec(memory_space=pl.ANY),
            pl.BlockSpec((1, SUBLANES, tm), lambda j, c, ex, count: (j, 0, 0)),
            pl.BlockSpec((1, d, tf), lambda j, c, ex, count: (ex[j], 0, chunk(j, c, count))),
            pl.BlockSpec((1, d, tf), lambda j, c, ex, count: (ex[j], 0, chunk(j, c, count))),
            pl.BlockSpec((1, tf, d), lambda j, c, ex, count: (ex[j], chunk(j, c, count), 0)),
        ],
        out_specs=pl.BlockSpec(memory_space=pl.ANY),
        scratch_shapes=[pltpu.VMEM((2, tm, d), F32), pltpu.VMEM((tm, d), BF16),
                        pltpu.VMEM((tm, d), F32), pltpu.VMEM((2, tm, d), F32),
                        pltpu.SemaphoreType.DMA((2,)), pltpu.SemaphoreType.DMA((2,))],
    )
    return pl.pallas_call(
        functools.partial(_moe_ffn_kernel, tm=tm, n_chunks=n_chunks, n_pairs=n_pairs),
        grid_spec=grid_spec,
        out_shape=jax.ShapeDtypeStruct((n_pairs + 2 * tm, d), F32),
        compiler_params=pltpu.CompilerParams(
            dimension_semantics=("arbitrary", "arbitrary"),
            vmem_limit_bytes=VMEM_LIMIT_BYTES),
        name="ffn_moe",
    )(tile_expert, tile_rows, src_token, src_token, dst_row, f, gate_row, w1, w3, w2)


def _ple_kernel(h1_ref, y0_ref, y1_ref, p_ref, pg_ref, wg_ref, wp_ref, o_ref):
    h2 = h1_ref[...] + y0_ref[...] + y1_ref[...]
    o_ref[...] = _ple(h2, p_ref[0, 0], pg_ref[...], wg_ref[...], wp_ref[...])


def _ple_combine(h1, y, p, layer, pg, wg, wp, *, tm):
    n, d = h1.shape
    row = lambda i: (i, 0)
    fixed = lambda i: (0, 0)
    return pl.pallas_call(
        _ple_kernel,
        grid=(n // tm,),
        in_specs=[
            pl.BlockSpec((tm, d), row),
            pl.BlockSpec((tm, d), row),
            pl.BlockSpec((tm, d), lambda i: (i + n // tm, 0)),
            pl.BlockSpec(*_layer_rows(p, layer, tm)),
            pl.BlockSpec((1, d), fixed),
            pl.BlockSpec(wg.shape, fixed),
            pl.BlockSpec(wp.shape, fixed),
        ],
        out_specs=pl.BlockSpec((tm, d), row),
        out_shape=jax.ShapeDtypeStruct((n, d), F32),
        compiler_params=pltpu.CompilerParams(
            dimension_semantics=("arbitrary",), vmem_limit_bytes=VMEM_LIMIT_BYTES),
        name="ple_combine",
    )(h1, y, y, p, pg, wg, wp)


def kernel(x, p, mix_norm_g, w_in, q_norm_g, k_norm_g, conv_w, conv_b, attn_out_g, conv_out_g,
           w_o, ffn_norm_g, dense_w1, dense_w3, dense_w2, router_w, moe_w1, moe_w3, moe_w2,
           ple_norm_g, ple_gate_w, ple_proj_w):
    bsz, seq_len, d = x.shape
    depth = w_in.shape[0]
    sb = attn_out_g.shape[1]
    n_heads = sb // HEAD_DIM
    n_experts = router_w.shape[2]
    n = bsz * seq_len
    tm = min(ROW_TILE, seq_len)
    t = min(ATTN_TILE, seq_len)
    tm_ffn = min(FFN_ROW_TILE, seq_len)

    head_id = jnp.arange(sb) // HEAD_DIM
    head_mean = jnp.where(head_id[:, None] == head_id[None, :], 1.0 / HEAD_DIM, 0.0).astype(BF16)
    ar = jnp.arange(t)
    tri = (ar[:, None] >= ar[None, :]).astype(BF16)
    q_scale = LOG2E / float(np.sqrt(HEAD_DIM))
    row2 = lambda a: a.reshape(1, -1)

    h = x.reshape(n, d)
    for i in range(depth):
        qg = row2(jnp.tile(q_norm_g[i], n_heads) * q_scale)
        kg = row2(jnp.tile(k_norm_g[i], n_heads))
        q, k, v, cu, b = _mix_in(h, row2(mix_norm_g[i]), w_in[i].astype(BF16), qg, kg,
                                 head_mean, tm=tm, seq_len=seq_len)
        attn = _attention(q, k, v, tri, t=t, pairs=ATTN_PAIRS)
        j = i // 2
        routed = i % 2 == 1
        rw = None
        if routed:
            e_pad = -(-n_experts // SUBLANES) * SUBLANES
            rw = jnp.pad(router_w[j].T, ((0, e_pad - n_experts), (0, 0)))
            rw_hi = rw.astype(BF16)
            rw = jnp.stack([rw_hi, (rw - rw_hi.astype(F32)).astype(BF16)])
        outs = _mix_out(attn, cu, b, h, row2(attn_out_g[i]), row2(conv_out_g[i]), conv_w[i],
                        row2(conv_b[i]), w_o[i].astype(BF16), row2(ffn_norm_g[i]), rw,
                        tm=tm, seq_len=seq_len, n_experts=n_experts if routed else 0)
        ple_w = (row2(ple_norm_g[i]), ple_gate_w[i].astype(BF16), ple_proj_w[i].astype(BF16))
        if routed:
            h1, f, route = outs
            plan = _route_plan(route, n_experts, MOE_ROW_TILE)
            y = _moe_ffn(f, plan, moe_w1[j].astype(BF16), moe_w3[j].astype(BF16),
                         moe_w2[j].astype(BF16), tm=MOE_ROW_TILE, tf=MOE_COL_TILE)
            h = _ple_combine(h1, y, p, i, *ple_w, tm=tm)
        else:
            h1, f = outs
            h = _ffn(f, dense_w1[j:j + 1].astype(BF16), dense_w3[j:j + 1].astype(BF16),
                     dense_w2[j:j + 1].astype(BF16), h1, p, i, *ple_w, tm=tm_ffn,
                     tf=FFN_COL_TILE)
    return h.reshape(bsz, seq_len, d)
```

```python
import functools

import numpy as np
import jax
import jax.numpy as jnp
from jax import lax
from jax.experimental import pallas as pl
from jax.experimental.pallas import tpu as pltpu

F32 = jnp.float32
BF16 = jnp.bfloat16

EPS = 1e-6
HEAD_DIM = 64
CONV_K = 3
TOP_K = 2
LANES = 128
SUBLANES = 8
HALO_ROWS = 2 * SUBLANES
LOG2E = float(np.log2(np.e))

VMEM_LIMIT_BYTES = 56 * 1024 * 1024

ROW_TILE = 1024
ATTN_TILE = 256
ATTN_PAIRS = 4
SP_CLAMP = 64.0
DONE_LOG2 = 160.0
NO_BLOCK_DECAY = 1e30
FFN_ROW_TILE = 512
FFN_COL_TILE = 1792
MOE_ROW_TILE = 512
MOE_COL_TILE = 1792


def _rms(x, g):
    return x * lax.rsqrt(jnp.mean(x * x, axis=-1, keepdims=True) + EPS) * g


def _dot(a, b):
    return jnp.dot(a, b, preferred_element_type=F32)


def _mix_in_kernel(h_ref, g_ref, w_ref, qg_ref, kg_ref, hm_ref,
                   q_ref, k_ref, v_ref, cu_ref, b_ref, *, sb):
    a = _rms(h_ref[...], g_ref[...]).astype(BF16)
    hm = hm_ref[...]

    def qk_norm(t, g):
        ms = _dot((t * t).astype(BF16), hm)
        return (t * lax.rsqrt(ms + EPS) * g).astype(BF16)

    q_ref[0] = qk_norm(_dot(a, w_ref[:, 0:sb]), qg_ref[...])
    k_ref[0] = qk_norm(_dot(a, w_ref[:, sb:2 * sb]), kg_ref[...])
    v_ref[0] = _dot(a, w_ref[:, 2 * sb:3 * sb]).astype(BF16)
    u = _dot(a, w_ref[:, 3 * sb:4 * sb])
    c = _dot(a, w_ref[:, 4 * sb:5 * sb])
    cu_ref[...] = (c * u).astype(BF16)
    b_ref[...] = _dot(a, w_ref[:, 5 * sb:6 * sb]).astype(BF16)


def _mix_in(h, g, w_in, qg, kg, head_mean, *, tm, seq_len):
    n, d = h.shape
    sb = head_mean.shape[0]
    tiles_per_seq = seq_len // tm
    row = lambda i: (i, 0)
    fixed = lambda i: (0, 0)
    seq_row = lambda i: (i // tiles_per_seq, i % tiles_per_seq, 0)
    out_bf = jax.ShapeDtypeStruct((n // seq_len, seq_len, sb), BF16)
    out_rows = jax.ShapeDtypeStruct((n, sb), BF16)
    return pl.pallas_call(
        functools.partial(_mix_in_kernel, sb=sb),
        grid=(n // tm,),
        in_specs=[
            pl.BlockSpec((tm, d), row),
            pl.BlockSpec((1, d), fixed),
            pl.BlockSpec(w_in.shape, fixed),
            pl.BlockSpec((1, sb), fixed),
            pl.BlockSpec((1, sb), fixed),
            pl.BlockSpec((sb, sb), fixed),
        ],
        out_specs=[pl.BlockSpec((1, tm, sb), seq_row)] * 3 + [pl.BlockSpec((tm, sb), row)] * 2,
        out_shape=[out_bf, out_bf, out_bf, out_rows, out_rows],
        compiler_params=pltpu.CompilerParams(
            dimension_semantics=("arbitrary",), vmem_limit_bytes=VMEM_LIMIT_BYTES),
        name="mix_in",
    )(h, g, w_in, qg, kg, head_mean)


def _attn_kernel(q_ref, k_ref, v_ref, u_ref, o_ref, acc_ref, c_ref, *, t, pairs):
    i = pl.program_id(2)
    heads = 2 * pairs
    q = q_ref[0]
    lane = lax.broadcasted_iota(jnp.int32, (t, LANES), 1)
    lo_half = lane < HEAD_DIM
    q_heads = []
    for p in range(pairs):
        qp = q[:, p * LANES:(p + 1) * LANES]
        q_heads += [jnp.where(lo_half, qp, jnp.zeros_like(qp)),
                    jnp.where(lo_half, jnp.zeros_like(qp), qp)]
    tri = u_ref[...]
    row = lax.broadcasted_iota(jnp.int32, (t, t), 0)
    col = lax.broadcasted_iota(jnp.int32, (t, t), 1)
    causal = col < row

    acc_ref[...] = jnp.zeros_like(acc_ref)
    c_ref[...] = jnp.zeros_like(c_ref)

    def block(j, diagonal, extra_decay=0.0):
        start = pl.multiple_of(j * t, t)
        kj = k_ref[0, pl.ds(start, t), :]
        vj = v_ref[0, pl.ds(start, t), :]
        for hd in range(heads):
            p = hd // 2
            z = lax.dot_general(q_heads[hd], kj[:, p * LANES:(p + 1) * LANES],
                                (((1,), (1,)), ((), ())), preferred_element_type=F32)
            sp = jnp.maximum(z, jnp.log2(1.0 + jnp.exp2(jnp.minimum(z, SP_CLAMP))))
            if diagonal:
                sp = jnp.where(causal, sp, 0.0)
            incl = _dot(sp.astype(BF16), tri)
            w = jnp.exp2(z - incl - (c_ref[hd] + extra_decay))
            if diagonal:
                w = jnp.where(causal, w, 0.0)
            acc_ref[hd] += _dot(w.astype(BF16), vj[:, p * LANES:(p + 1) * LANES])
            c_ref[hd] += incl[:, 0:1]

    def decay_floor():
        m = jnp.min(c_ref[0])
        for hd in range(1, heads):
            m = jnp.minimum(m, jnp.min(c_ref[hd]))
        return m

    block(i, True)
    block(jnp.maximum(i - 1, 0), False, extra_decay=jnp.where(i > 0, 0.0, NO_BLOCK_DECAY))

    def cond(carry):
        j, floor = carry
        return (j >= 0) & (floor < DONE_LOG2)

    def body(carry):
        j, _ = carry
        block(j, False)
        return j - 1, decay_floor()

    lax.while_loop(cond, body, (i - 2, decay_floor()))
    for p in range(pairs):
        o_ref[0, :, p * LANES:(p + 1) * LANES] = jnp.where(
            lo_half, acc_ref[2 * p], acc_ref[2 * p + 1]).astype(o_ref.dtype)


def _attention(q, k, v, tri, *, t, pairs):
    bsz, s, sb = q.shape
    width = LANES * pairs
    blk = lambda b, g, i: (b, i, g)
    seq = lambda b, g, i: (b, 0, g)
    return pl.pallas_call(
        functools.partial(_attn_kernel, t=t, pairs=pairs),
        grid=(bsz, sb // width, s // t),
        in_specs=[
            pl.BlockSpec((1, t, width), blk),
            pl.BlockSpec((1, s, width), seq),
            pl.BlockSpec((1, s, width), seq),
            pl.BlockSpec((t, t), lambda b, g, i: (0, 0)),
        ],
        out_specs=pl.BlockSpec((1, t, width), blk),
        out_shape=jax.ShapeDtypeStruct((bsz, s, sb), BF16),
        scratch_shapes=[pltpu.VMEM((2 * pairs, t, LANES), F32),
                        pltpu.VMEM((2 * pairs, t, 1), F32)],
        compiler_params=pltpu.CompilerParams(
            dimension_semantics=("arbitrary", "arbitrary", "arbitrary"),
            vmem_limit_bytes=VMEM_LIMIT_BYTES),
        name="attention",
    )(q, k, v, tri)


def _mix_out_kernel(*refs, tiles_per_seq, sb, n_experts):
    (attn_ref, cu_ref, halo_ref, b_ref, h_ref, ag_ref, cg_ref, cw_ref, cb_ref,
     wo_ref, fg_ref) = refs[:11]
    if n_experts:
        rw_ref, h1_ref, f_ref, route_ref = refs[11:]
    else:
        h1_ref, f_ref = refs[11:]

    attn_n = _rms(attn_ref[0].astype(F32), ag_ref[...])

    cu = cu_ref[...].astype(F32)
    first = pl.program_id(0) % tiles_per_seq == 0
    halo = jnp.where(first, 0.0, halo_ref[...].astype(F32))
    rowi = lax.broadcasted_iota(jnp.int32, cu.shape, 0)
    prev1 = jnp.where(rowi == 0, halo[HALO_ROWS - 1:HALO_ROWS], pltpu.roll(cu, 1, axis=0))
    prev2 = jnp.where(rowi == 0, halo[HALO_ROWS - 2:HALO_ROWS - 1],
                      jnp.where(rowi == 1, halo[HALO_ROWS - 1:HALO_ROWS],
                                pltpu.roll(cu, 2, axis=0)))
    conv = cw_ref[0:1] * prev2 + cw_ref[1:2] * prev1 + cw_ref[2:3] * cu + cb_ref[...]
    conv_n = _rms(b_ref[...].astype(F32) * conv, cg_ref[...])

    h1 = (h_ref[...] + _dot(attn_n.astype(BF16), wo_ref[0:sb])
          + _dot(conv_n.astype(BF16), wo_ref[sb:]))
    h1_ref[...] = h1
    f = _rms(h1, fg_ref[...])
    f_ref[...] = f.astype(f_ref.dtype)

    if n_experts:
        f_hi = f.astype(BF16)
        f_lo = (f - f_hi.astype(F32)).astype(BF16)
        nt = (((1,), (1,)), ((), ()))
        logits = (lax.dot_general(rw_ref[0], f_hi, nt, preferred_element_type=F32)
                  + (lax.dot_general(rw_ref[0], f_lo, nt, preferred_element_type=F32)
                     + lax.dot_general(rw_ref[1], f_hi, nt, preferred_element_type=F32)))
        e_pad = logits.shape[0]
        eid = lax.broadcasted_iota(jnp.int32, logits.shape, 0).astype(F32)
        neg = jnp.float32(-jnp.inf)
        logits = jnp.where(eid < n_experts, logits, neg)
        m1 = jnp.max(logits, axis=0, keepdims=True)
        i1 = jnp.min(jnp.where(logits == m1, eid, float(e_pad)), axis=0, keepdims=True)
        rest = jnp.where(eid == i1, neg, logits)
        m2 = jnp.max(rest, axis=0, keepdims=True)
        i2 = jnp.min(jnp.where(rest == m2, eid, float(e_pad)), axis=0, keepdims=True)
        e2 = jnp.exp(m2 - m1)
        g1 = 1.0 / (1.0 + e2)
        g2 = e2 * g1
        slot = lax.broadcasted_iota(jnp.int32, route_ref.shape, 0)
        route_ref[...] = jnp.where(
            slot == 0, i1, jnp.where(slot == 1, i2, jnp.where(
                slot == 2, g1, jnp.where(slot == 3, g2, 0.0))))


def _mix_out(attn, cu, b, h, ag, cg, cw, cb, w_o, fg, router_w, *, tm, seq_len, n_experts):
    n, d = h.shape
    sb = attn.shape[2]
    tiles_per_seq = seq_len // tm
    row = lambda i: (i, 0)
    fixed = lambda i: (0, 0)
    halo = lambda i: (jnp.maximum(i * (tm // HALO_ROWS) - 1, 0), 0)
    in_specs = [
        pl.BlockSpec((1, tm, sb), lambda i: (i // tiles_per_seq, i % tiles_per_seq, 0)),
        pl.BlockSpec((tm, sb), row),
        pl.BlockSpec((HALO_ROWS, sb), halo),
        pl.BlockSpec((tm, sb), row),
        pl.BlockSpec((tm, d), row),
        pl.BlockSpec((1, sb), fixed),
        pl.BlockSpec((1, sb), fixed),
        pl.BlockSpec((CONV_K, sb), fixed),
        pl.BlockSpec((1, sb), fixed),
        pl.BlockSpec(w_o.shape, fixed),
        pl.BlockSpec((1, d), fixed),
    ]
    args = [attn, cu, cu, b, h, ag, cg, cw, cb, w_o, fg]
    out_specs = [pl.BlockSpec((tm, d), row), pl.BlockSpec((tm, d), row)]
    out_shape = [jax.ShapeDtypeStruct((n, d), F32),
                 jax.ShapeDtypeStruct((n, d), F32 if n_experts else BF16)]
    if n_experts:
        in_specs.append(pl.BlockSpec(router_w.shape, lambda i: (0, 0, 0)))
        args.append(router_w)
        out_specs.append(pl.BlockSpec((SUBLANES, tm), lambda i: (0, i)))
        out_shape.append(jax.ShapeDtypeStruct((SUBLANES, n), F32))
    return pl.pallas_call(
        functools.partial(_mix_out_kernel, tiles_per_seq=seq_len // tm, sb=sb,
                          n_experts=n_experts),
        grid=(n // tm,),
        in_specs=in_specs,
        out_specs=out_specs,
        out_shape=out_shape,
        compiler_params=pltpu.CompilerParams(
            dimension_semantics=("arbitrary",), vmem_limit_bytes=VMEM_LIMIT_BYTES),
        name="mix_out_moe" if n_experts else "mix_out",
    )(*args)


def _swiglu_step(x, w1_ref, w3_ref, w2_ref):
    a = _dot(x, w1_ref[0])
    hidden = a * jax.nn.sigmoid(a) * _dot(x, w3_ref[0])
    return _dot(hidden.astype(BF16), w2_ref[0])


def _ple(h2, p, pg, wg, wp):
    gate = jax.nn.sigmoid(_dot(_rms(h2, pg).astype(BF16), wg))
    return h2 + gate * _dot(p.astype(BF16), wp)


def _ffn_kernel(f_ref, w1_ref, w3_ref, w2_ref, h1_ref, p_ref, pg_ref, wg_ref, wp_ref,
                o_ref, acc_ref):
    c = pl.program_id(1)

    @pl.when(c == 0)
    def _():
        acc_ref[...] = jnp.zeros_like(acc_ref)

    acc_ref[...] += _swiglu_step(f_ref[...], w1_ref, w3_ref, w2_ref)

    @pl.when(c == pl.num_programs(1) - 1)
    def _():
        o_ref[...] = _ple(h1_ref[...] + acc_ref[...], p_ref[0, 0], pg_ref[...], wg_ref[...],
                          wp_ref[...])


def _layer_rows(p, layer, tm):
    tiles_per_seq = p.shape[2] // tm
    return ((1, 1, tm, p.shape[3]),
            lambda i, *_: (layer, i // tiles_per_seq, i % tiles_per_seq, 0))


def _ffn(f, w1, w3, w2, h1, p, layer, pg, wg, wp, *, tm, tf):
    n, d = h1.shape
    d_ff = w1.shape[2]
    row = lambda i, c: (i, 0)
    fixed = lambda i, c: (0, 0)
    in_specs = [
        pl.BlockSpec((tm, d), row),
        pl.BlockSpec((1, d, tf), lambda i, c: (0, 0, c)),
        pl.BlockSpec((1, d, tf), lambda i, c: (0, 0, c)),
        pl.BlockSpec((1, tf, d), lambda i, c: (0, c, 0)),
        pl.BlockSpec((tm, d), row),
        pl.BlockSpec(*_layer_rows(p, layer, tm)),
        pl.BlockSpec((1, d), fixed),
        pl.BlockSpec(wg.shape, fixed),
        pl.BlockSpec(wp.shape, fixed),
    ]
    return pl.pallas_call(
        _ffn_kernel,
        grid=(n // tm, d_ff // tf),
        in_specs=in_specs,
        out_specs=pl.BlockSpec((tm, d), row),
        out_shape=jax.ShapeDtypeStruct((n, d), F32),
        scratch_shapes=[pltpu.VMEM((tm, d), F32)],
        compiler_params=pltpu.CompilerParams(
            dimension_semantics=("arbitrary", "arbitrary"),
            vmem_limit_bytes=VMEM_LIMIT_BYTES),
        name="ffn_dense",
    )(f, w1, w3, w2, h1, p, pg, wg, wp)


def _route_plan(route, n_experts, tm):
    n = route.shape[1]
    pairs = TOP_K * n
    n_rows = pairs + n_experts * tm
    eid = route[0:TOP_K].astype(jnp.int32).reshape(pairs)
    gates = route[TOP_K:2 * TOP_K].reshape(pairs)
    pair_bits = int(np.ceil(np.log2(pairs)))
    keys = jnp.sort(eid * (1 << pair_bits) + jnp.arange(pairs, dtype=jnp.int32))
    order = keys & ((1 << pair_bits) - 1)
    bounds = jnp.arange(n_experts + 1, dtype=jnp.int32) << pair_bits
    bounds = bounds.at[n_experts].set(jnp.iinfo(jnp.int32).max)
    first = jnp.sum(keys[None, :] < bounds[:, None], axis=1).astype(jnp.int32)
    counts = first[1:] - first[:-1]
    padded = (counts + tm - 1) // tm * tm
    ends = jnp.cumsum(padded)
    n_tiles = n_rows // tm
    tile_start = jnp.arange(n_tiles, dtype=jnp.int32) * tm
    tile_expert = jnp.sum(tile_start[:, None] >= ends[None, :], axis=1)
    tile_expert = jnp.minimum(tile_expert, n_experts - 1).astype(jnp.int32)
    in_group = tile_start - (ends - padded)[tile_expert]
    tile_rows = jnp.clip(counts[tile_expert] - in_group, 0, tm)
    lane = jnp.arange(tm, dtype=jnp.int32)[None, :]
    real = lane < tile_rows[:, None]
    safe = order[jnp.clip((first[tile_expert] + in_group)[:, None] + lane, 0, pairs - 1)]
    src_token = jnp.where(real, safe % n, 0)
    gate_row = jnp.where(real, gates[safe], 0.0)
    spare = pairs + (jnp.arange(n_tiles, dtype=jnp.int32) % 2 * tm)[:, None] + lane
    dst_row = jnp.where(real, safe, spare)
    src_token = jnp.concatenate([src_token, jnp.zeros((1, tm), jnp.int32)])
    dst_row = jnp.concatenate([pairs + tm + lane, dst_row])
    return (src_token.reshape(n_tiles + 1, 1, tm), dst_row.reshape(n_tiles + 1, 1, tm),
            jnp.broadcast_to(gate_row[:, None, :], (n_tiles, SUBLANES, tm)), tile_expert,
            tile_rows)


def _for_each_row(count, fn, unroll=8):
    assert count % unroll == 0

    def chunk(i, carry):
        for u in range(unroll):
            fn(i * unroll + u)
        return carry

    lax.fori_loop(0, count // unroll, chunk, 0)


def _moe_ffn_kernel(expert_ref, count_ref, src0_ref, src_ref, dst_ref, f_hbm, gate_ref,
                    w1_ref, w3_ref, w2_ref, y_hbm, rows_ref, x_ref, acc_ref, stage_ref,
                    gather_sem, scatter_sem, *, tm, n_chunks, n_pairs):
    j = pl.program_id(0)
    c = pl.program_id(1)
    last_tile = pl.num_programs(0) - 1
    last_chunk = n_chunks - 1
    per_step = tm // n_chunks
    active = count_ref[j] > 0
    cur = j % 2
    other = 1 - cur

    def tile_gather(slot):
        return pltpu.make_async_copy(f_hbm.at[pl.ds(0, tm), :], rows_ref.at[slot],
                                     gather_sem.at[slot])

    def tile_scatter(slot):
        return pltpu.make_async_copy(stage_ref.at[slot], y_hbm.at[pl.ds(0, tm), :],
                                     scatter_sem.at[slot])

    @pl.when((j == 0) & (c == 0))
    def _():
        stage_ref[...] = jnp.zeros_like(stage_ref)
        for slot in range(2):
            spare = pltpu.make_async_copy(stage_ref.at[slot],
                                          y_hbm.at[pl.ds(n_pairs + slot * tm, tm), :],
                                          scatter_sem.at[slot])
            spare.start()
            spare.wait()

        def gather_first(r):
            pltpu.make_async_copy(f_hbm.at[pl.ds(src0_ref[0, 0, r], 1), :],
                                  rows_ref.at[0, pl.ds(r, 1), :], gather_sem.at[0]).start()
        _for_each_row(tm, gather_first)

    @pl.when(c == 0)
    def _():
        tile_gather(cur).wait()

        @pl.when(j >= 1)
        def _():
            tile_scatter(cur).wait()

        x_ref[...] = rows_ref[cur].astype(BF16)
        acc_ref[...] = jnp.zeros_like(acc_ref)

    def start_row_copies():
        base = pl.multiple_of(c * per_step, per_step)
        for r in range(per_step):
            pltpu.make_async_copy(f_hbm.at[pl.ds(src_ref[0, 0, base + r], 1), :],
                                  rows_ref.at[other, pl.ds(base + r, 1), :],
                                  gather_sem.at[other]).start()
            pltpu.make_async_copy(stage_ref.at[other, pl.ds(base + r, 1), :],
                                  y_hbm.at[pl.ds(dst_ref[0, 0, base + r], 1), :],
                                  scatter_sem.at[other]).start()

    @pl.when(active)
    def _():
        start_row_copies()
        acc_ref[...] += _swiglu_step(x_ref[...], w1_ref, w3_ref, w2_ref)

    @pl.when(jnp.logical_not(active))
    def _():
        start_row_copies()

    @pl.when(active & (c == last_chunk))
    def _():
        gate = jnp.transpose(gate_ref[0])[:, 0:1]
        stage_ref[cur] = acc_ref[...] * gate

    @pl.when((j == last_tile) & (c == last_chunk))
    def _():
        tile_gather(other).wait()
        tile_scatter(other).wait()


def _moe_ffn(f, plan, w1, w3, w2, *, tm, tf):
    src_token, dst_row, gate_row, tile_expert, tile_rows = plan
    n, d = f.shape
    n_exp, _, d_ff = w1.shape
    n_tiles = tile_rows.shape[0]
    n_chunks = d_ff // tf
    n_pairs = TOP_K * n
    assert n_pairs % tm == 0 and tm % n_chunks == 0 and n_exp < tm
    chunk = lambda j, c, count: jnp.where(count[j] > 0, c, n_chunks - 1)
    grid_spec = pltpu.PrefetchScalarGridSpec(
        num_scalar_prefetch=2,
        grid=(n_tiles, n_chunks),
        in_specs=[
            pl.BlockSpec((1, 1, tm), lambda j, c, ex, count: (0, 0, 0), memory_space=pltpu.SMEM),
            pl.BlockSpec((1, 1, tm), lambda j, c, ex, count: (j + 1, 0, 0),
                         memory_space=pltpu.SMEM),
            pl.BlockSpec((1, 1, tm), lambda j, c, ex, count: (j, 0, 0), memory_space=pltpu.SMEM),
            pl.BlockSpec(memory_space=pl.ANY),
            pl.BlockSpec((1, SUBLANES, tm), lambda j, c, ex, count: (j, 0, 0)),
            pl.BlockSpec((1, d, tf), lambda j, c, ex, count: (ex[j], 0, chunk(j, c, count))),
            pl.BlockSpec((1, d, tf), lambda j, c, ex, count: (ex[j], 0, chunk(j, c, count))),
            pl.BlockSpec((1, tf, d), lambda j, c, ex, count: (ex[j], chunk(j, c, count), 0)),
        ],
        out_specs=pl.BlockSpec(memory_space=pl.ANY),
        scratch_shapes=[pltpu.VMEM((2, tm, d), F32), pltpu.VMEM((tm, d), BF16),
                        pltpu.VMEM((tm, d), F32), pltpu.VMEM((2, tm, d), F32),
                        pltpu.SemaphoreType.DMA((2,)), pltpu.SemaphoreType.DMA((2,))],
    )
    return pl.pallas_call(
        functools.partial(_moe_ffn_kernel, tm=tm, n_chunks=n_chunks, n_pairs=n_pairs),
        grid_spec=grid_spec,
        out_shape=jax.ShapeDtypeStruct((n_pairs + 2 * tm, d), F32),
        compiler_params=pltpu.CompilerParams(
            dimension_semantics=("arbitrary", "arbitrary"),
            vmem_limit_bytes=VMEM_LIMIT_BYTES),
        name="ffn_moe",
    )(tile_expert, tile_rows, src_token, src_token, dst_row, f, gate_row, w1, w3, w2)


def _ple_kernel(h1_ref, y0_ref, y1_ref, p_ref, pg_ref, wg_ref, wp_ref, o_ref):
    h2 = h1_ref[...] + y0_ref[...] + y1_ref[...]
    o_ref[...] = _ple(h2, p_ref[0, 0], pg_ref[...], wg_ref[...], wp_ref[...])


def _ple_combine(h1, y, p, layer, pg, wg, wp, *, tm):
    n, d = h1.shape
    row = lambda i: (i, 0)
    fixed = lambda i: (0, 0)
    return pl.pallas_call(
        _ple_kernel,
        grid=(n // tm,),
        in_specs=[
            pl.BlockSpec((tm, d), row),
            pl.BlockSpec((tm, d), row),
            pl.BlockSpec((tm, d), lambda i: (i + n // tm, 0)),
            pl.BlockSpec(*_layer_rows(p, layer, tm)),
            pl.BlockSpec((1, d), fixed),
            pl.BlockSpec(wg.shape, fixed),
            pl.BlockSpec(wp.shape, fixed),
        ],
        out_specs=pl.BlockSpec((tm, d), row),
        out_shape=jax.ShapeDtypeStruct((n, d), F32),
        compiler_params=pltpu.CompilerParams(
            dimension_semantics=("arbitrary",), vmem_limit_bytes=VMEM_LIMIT_BYTES),
        name="ple_combine",
    )(h1, y, y, p, pg, wg, wp)


def kernel(x, p, mix_norm_g, w_in, q_norm_g, k_norm_g, conv_w, conv_b, attn_out_g, conv_out_g,
           w_o, ffn_norm_g, dense_w1, dense_w3, dense_w2, router_w, moe_w1, moe_w3, moe_w2,
           ple_norm_g, ple_gate_w, ple_proj_w):
    bsz, seq_len, d = x.shape
    depth = w_in.shape[0]
    sb = attn_out_g.shape[1]
    n_heads = sb // HEAD_DIM
    n_experts = router_w.shape[2]
    n = bsz * seq_len
    tm = min(ROW_TILE, seq_len)
    t = min(ATTN_TILE, seq_len)
    tm_ffn = min(FFN_ROW_TILE, seq_len)

    head_id = jnp.arange(sb) // HEAD_DIM
    head_mean = jnp.where(head_id[:, None] == head_id[None, :], 1.0 / HEAD_DIM, 0.0).astype(BF16)
    ar = jnp.arange(t)
    tri = (ar[:, None] >= ar[None, :]).astype(BF16)
    q_scale = LOG2E / float(np.sqrt(HEAD_DIM))
    row2 = lambda a: a.reshape(1, -1)

    h = x.reshape(n, d)
    for i in range(depth):
        qg = row2(jnp.tile(q_norm_g[i], n_heads) * q_scale)
        kg = row2(jnp.tile(k_norm_g[i], n_heads))
        q, k, v, cu, b = _mix_in(h, row2(mix_norm_g[i]), w_in[i].astype(BF16), qg, kg,
                                 head_mean, tm=tm, seq_len=seq_len)
        attn = _attention(q, k, v, tri, t=t, pairs=ATTN_PAIRS)
        j = i // 2
        routed = i % 2 == 1
        rw = None
        if routed:
            e_pad = -(-n_experts // SUBLANES) * SUBLANES
            rw = jnp.pad(router_w[j].T, ((0, e_pad - n_experts), (0, 0)))
            rw_hi = rw.astype(BF16)
            rw = jnp.stack([rw_hi, (rw - rw_hi.astype(F32)).astype(BF16)])
        outs = _mix_out(attn, cu, b, h, row2(attn_out_g[i]), row2(conv_out_g[i]), conv_w[i],
                        row2(conv_b[i]), w_o[i].astype(BF16), row2(ffn_norm_g[i]), rw,
                        tm=tm, seq_len=seq_len, n_experts=n_experts if routed else 0)
        ple_w = (row2(ple_norm_g[i]), ple_gate_w[i].astype(BF16), ple_proj_w[i].astype(BF16))
        if routed:
            h1, f, route = outs
            plan = _route_plan(route, n_experts, MOE_ROW_TILE)
            y = _moe_ffn(f, plan, moe_w1[j].astype(BF16), moe_w3[j].astype(BF16),
                         moe_w2[j].astype(BF16), tm=MOE_ROW_TILE, tf=MOE_COL_TILE)
            h = _ple_combine(h1, y, p, i, *ple_w, tm=tm)
        else:
            h1, f = outs
            h = _ffn(f, dense_w1[j:j + 1].astype(BF16), dense_w3[j:j + 1].astype(BF16),
                     dense_w2[j:j + 1].astype(BF16), h1, p, i, *ple_w, tm=tm_ffn,
                     tf=FFN_COL_TILE)
    return h.reshape(bsz, seq_len, d)
```

```python
import functools

import numpy as np
import jax
import jax.numpy as jnp
from jax import lax
from jax.experimental import pallas as pl
from jax.experimental.pallas import tpu as pltpu

F32 = jnp.float32
BF16 = jnp.bfloat16

EPS = 1e-6
HEAD_DIM = 64
CONV_K = 3
TOP_K = 2
LANES = 128
SUBLANES = 8
HALO_ROWS = 2 * SUBLANES
LOG2E = float(np.log2(np.e))

VMEM_LIMIT_BYTES = 56 * 1024 * 1024

ROW_TILE = 1024
ATTN_TILE = 256
ATTN_PAIRS = 4
SP_CLAMP = 64.0
DONE_LOG2 = 160.0
NO_BLOCK_DECAY = 1e30
FFN_ROW_TILE = 512
FFN_COL_TILE = 3584
MOE_ROW_TILE = 512
MOE_COL_TILE = 1792


def _rms(x, g):
    return x * lax.rsqrt(jnp.mean(x * x, axis=-1, keepdims=True) + EPS) * g


def _dot(a, b):
    return jnp.dot(a, b, preferred_element_type=F32)


def _mix_in_kernel(h_ref, g_ref, w_ref, qg_ref, kg_ref, hm_ref,
                   q_ref, k_ref, v_ref, cu_ref, b_ref, *, sb):
    a = _rms(h_ref[...], g_ref[...]).astype(BF16)
    hm = hm_ref[...]

    def qk_norm(t, g):
        ms = _dot((t * t).astype(BF16), hm)
        return (t * lax.rsqrt(ms + EPS) * g).astype(BF16)

    q_ref[0] = qk_norm(_dot(a, w_ref[:, 0:sb]), qg_ref[...])
    k_ref[0] = qk_norm(_dot(a, w_ref[:, sb:2 * sb]), kg_ref[...])
    v_ref[0] = _dot(a, w_ref[:, 2 * sb:3 * sb]).astype(BF16)
    u = _dot(a, w_ref[:, 3 * sb:4 * sb])
    c = _dot(a, w_ref[:, 4 * sb:5 * sb])
    cu_ref[...] = (c * u).astype(BF16)
    b_ref[...] = _dot(a, w_ref[:, 5 * sb:6 * sb]).astype(BF16)


def _mix_in(h, g, w_in, qg, kg, head_mean, *, tm, seq_len):
    n, d = h.shape
    sb = head_mean.shape[0]
    tiles_per_seq = seq_len // tm
    row = lambda i: (i, 0)
    fixed = lambda i: (0, 0)
    seq_row = lambda i: (i // tiles_per_seq, i % tiles_per_seq, 0)
    out_bf = jax.ShapeDtypeStruct((n // seq_len, seq_len, sb), BF16)
    out_rows = jax.ShapeDtypeStruct((n, sb), BF16)
    return pl.pallas_call(
        functools.partial(_mix_in_kernel, sb=sb),
        grid=(n // tm,),
        in_specs=[
            pl.BlockSpec((tm, d), row),
            pl.BlockSpec((1, d), fixed),
            pl.BlockSpec(w_in.shape, fixed),
            pl.BlockSpec((1, sb), fixed),
            pl.BlockSpec((1, sb), fixed),
            pl.BlockSpec((sb, sb), fixed),
        ],
        out_specs=[pl.BlockSpec((1, tm, sb), seq_row)] * 3 + [pl.BlockSpec((tm, sb), row)] * 2,
        out_shape=[out_bf, out_bf, out_bf, out_rows, out_rows],
        compiler_params=pltpu.CompilerParams(
            dimension_semantics=("arbitrary",), vmem_limit_bytes=VMEM_LIMIT_BYTES),
        name="mix_in",
    )(h, g, w_in, qg, kg, head_mean)


def _attn_kernel(q_ref, k_ref, v_ref, u_ref, o_ref, acc_ref, c_ref, *, t, pairs):
    i = pl.program_id(2)
    heads = 2 * pairs
    q = q_ref[0]
    lane = lax.broadcasted_iota(jnp.int32, (t, LANES), 1)
    lo_half = lane < HEAD_DIM
    q_heads = []
    for p in range(pairs):
        qp = q[:, p * LANES:(p + 1) * LANES]
        q_heads += [jnp.where(lo_half, qp, jnp.zeros_like(qp)),
                    jnp.where(lo_half, jnp.zeros_like(qp), qp)]
    tri = u_ref[...]
    row = lax.broadcasted_iota(jnp.int32, (t, t), 0)
    col = lax.broadcasted_iota(jnp.int32, (t, t), 1)
    causal = col < row

    acc_ref[...] = jnp.zeros_like(acc_ref)
    c_ref[...] = jnp.zeros_like(c_ref)

    def block(j, diagonal, extra_decay=0.0):
        start = pl.multiple_of(j * t, t)
        kj = k_ref[0, pl.ds(start, t), :]
        vj = v_ref[0, pl.ds(start, t), :]
        for hd in range(heads):
            p = hd // 2
            z = lax.dot_general(q_heads[hd], kj[:, p * LANES:(p + 1) * LANES],
                                (((1,), (1,)), ((), ())), preferred_element_type=F32)
            sp = jnp.maximum(z, jnp.log2(1.0 + jnp.exp2(jnp.minimum(z, SP_CLAMP))))
            if diagonal:
                sp = jnp.where(causal, sp, 0.0)
            incl = _dot(sp.astype(BF16), tri)
            w = jnp.exp2(z - incl - (c_ref[hd] + extra_decay))
            if diagonal:
                w = jnp.where(causal, w, 0.0)
            acc_ref[hd] += _dot(w.astype(BF16), vj[:, p * LANES:(p + 1) * LANES])
            c_ref[hd] += incl[:, 0:1]

    def decay_floor():
        m = jnp.min(c_ref[0])
        for hd in range(1, heads):
            m = jnp.minimum(m, jnp.min(c_ref[hd]))
        return m

    block(i, True)
    block(jnp.maximum(i - 1, 0), False, extra_decay=jnp.where(i > 0, 0.0, NO_BLOCK_DECAY))

    def cond(carry):
        j, floor = carry
        return (j >= 0) & (floor < DONE_LOG2)

    def body(carry):
        j, _ = carry
        block(j, False)
        return j - 1, decay_floor()

    lax.while_loop(cond, body, (i - 2, decay_floor()))
    for p in range(pairs):
        o_ref[0, :, p * LANES:(p + 1) * LANES] = jnp.where(
            lo_half, acc_ref[2 * p], acc_ref[2 * p + 1]).astype(o_ref.dtype)


def _attention(q, k, v, tri, *, t, pairs):
    bsz, s, sb = q.shape
    width = LANES * pairs
    blk = lambda b, g, i: (b, i, g)
    seq = lambda b, g, i: (b, 0, g)
    return pl.pallas_call(
        functools.partial(_attn_kernel, t=t, pairs=pairs),
        grid=(bsz, sb // width, s // t),
        in_specs=[
            pl.BlockSpec((1, t, width), blk),
            pl.BlockSpec((1, s, width), seq),
            pl.BlockSpec((1, s, width), seq),
            pl.BlockSpec((t, t), lambda b, g, i: (0, 0)),
        ],
        out_specs=pl.BlockSpec((1, t, width), blk),
        out_shape=jax.ShapeDtypeStruct((bsz, s, sb), BF16),
        scratch_shapes=[pltpu.VMEM((2 * pairs, t, LANES), F32),
                        pltpu.VMEM((2 * pairs, t, 1), F32)],
        compiler_params=pltpu.CompilerParams(
            dimension_semantics=("arbitrary", "arbitrary", "arbitrary"),
            vmem_limit_bytes=VMEM_LIMIT_BYTES),
        name="attention",
    )(q, k, v, tri)


def _mix_out_kernel(*refs, tiles_per_seq, sb, n_experts):
    (attn_ref, cu_ref, halo_ref, b_ref, h_ref, ag_ref, cg_ref, cw_ref, cb_ref,
     wo_ref, fg_ref) = refs[:11]
    if n_experts:
        rw_ref, h1_ref, f_ref, route_ref = refs[11:]
    else:
        h1_ref, f_ref = refs[11:]

    attn_n = _rms(attn_ref[0].astype(F32), ag_ref[...])

    cu = cu_ref[...].astype(F32)
    first = pl.program_id(0) % tiles_per_seq == 0
    halo = jnp.where(first, 0.0, halo_ref[...].astype(F32))
    rowi = lax.broadcasted_iota(jnp.int32, cu.shape, 0)
    prev1 = jnp.where(rowi == 0, halo[HALO_ROWS - 1:HALO_ROWS], pltpu.roll(cu, 1, axis=0))
    prev2 = jnp.where(rowi == 0, halo[HALO_ROWS - 2:HALO_ROWS - 1],
                      jnp.where(rowi == 1, halo[HALO_ROWS - 1:HALO_ROWS],
                                pltpu.roll(cu, 2, axis=0)))
    conv = cw_ref[0:1] * prev2 + cw_ref[1:2] * prev1 + cw_ref[2:3] * cu + cb_ref[...]
    conv_n = _rms(b_ref[...].astype(F32) * conv, cg_ref[...])

    h1 = (h_ref[...] + _dot(attn_n.astype(BF16), wo_ref[0:sb])
          + _dot(conv_n.astype(BF16), wo_ref[sb:]))
    h1_ref[...] = h1
    f = _rms(h1, fg_ref[...])
    f_ref[...] = f.astype(f_ref.dtype)

    if n_experts:
        f_hi = f.astype(BF16)
        f_lo = (f - f_hi.astype(F32)).astype(BF16)
        nt = (((1,), (1,)), ((), ()))
        logits = (lax.dot_general(rw_ref[0], f_hi, nt, preferred_element_type=F32)
                  + (lax.dot_general(rw_ref[0], f_lo, nt, preferred_element_type=F32)
                     + lax.dot_general(rw_ref[1], f_hi, nt, preferred_element_type=F32)))
        e_pad = logits.shape[0]
        eid = lax.broadcasted_iota(jnp.int32, logits.shape, 0).astype(F32)
        neg = jnp.float32(-jnp.inf)
        logits = jnp.where(eid < n_experts, logits, neg)
        m1 = jnp.max(logits, axis=0, keepdims=True)
        i1 = jnp.min(jnp.where(logits == m1, eid, float(e_pad)), axis=0, keepdims=True)
        rest = jnp.where(eid == i1, neg, logits)
        m2 = jnp.max(rest, axis=0, keepdims=True)
        i2 = jnp.min(jnp.where(rest == m2, eid, float(e_pad)), axis=0, keepdims=True)
        e2 = jnp.exp(m2 - m1)
        g1 = 1.0 / (1.0 + e2)
        g2 = e2 * g1
        slot = lax.broadcasted_iota(jnp.int32, route_ref.shape, 0)
        route_ref[...] = jnp.where(
            slot == 0, i1, jnp.where(slot == 1, i2, jnp.where(
                slot == 2, g1, jnp.where(slot == 3, g2, 0.0))))


def _mix_out(attn, cu, b, h, ag, cg, cw, cb, w_o, fg, router_w, *, tm, seq_len, n_experts):
    n, d = h.shape
    sb = attn.shape[2]
    tiles_per_seq = seq_len // tm
    row = lambda i: (i, 0)
    fixed = lambda i: (0, 0)
    halo = lambda i: (jnp.maximum(i * (tm // HALO_ROWS) - 1, 0), 0)
    in_specs = [
        pl.BlockSpec((1, tm, sb), lambda i: (i // tiles_per_seq, i % tiles_per_seq, 0)),
        pl.BlockSpec((tm, sb), row),
        pl.BlockSpec((HALO_ROWS, sb), halo),
        pl.BlockSpec((tm, sb), row),
        pl.BlockSpec((tm, d), row),
        pl.BlockSpec((1, sb), fixed),
        pl.BlockSpec((1, sb), fixed),
        pl.BlockSpec((CONV_K, sb), fixed),
        pl.BlockSpec((1, sb), fixed),
        pl.BlockSpec(w_o.shape, fixed),
        pl.BlockSpec((1, d), fixed),
    ]
    args = [attn, cu, cu, b, h, ag, cg, cw, cb, w_o, fg]
    out_specs = [pl.BlockSpec((tm, d), row), pl.BlockSpec((tm, d), row)]
    out_shape = [jax.ShapeDtypeStruct((n, d), F32),
                 jax.ShapeDtypeStruct((n, d), F32 if n_experts else BF16)]
    if n_experts:
        in_specs.append(pl.BlockSpec(router_w.shape, lambda i: (0, 0, 0)))
        args.append(router_w)
        out_specs.append(pl.BlockSpec((SUBLANES, tm), lambda i: (0, i)))
        out_shape.append(jax.ShapeDtypeStruct((SUBLANES, n), F32))
    return pl.pallas_call(
        functools.partial(_mix_out_kernel, tiles_per_seq=seq_len // tm, sb=sb,
                          n_experts=n_experts),
        grid=(n // tm,),
        in_specs=in_specs,
        out_specs=out_specs,
        out_shape=out_shape,
        compiler_params=pltpu.CompilerParams(
            dimension_semantics=("arbitrary",), vmem_limit_bytes=VMEM_LIMIT_BYTES),
        name="mix_out_moe" if n_experts else "mix_out",
    )(*args)


def _swiglu_step(x, w1_ref, w3_ref, w2_ref):
    a = _dot(x, w1_ref[0])
    hidden = a * jax.nn.sigmoid(a) * _dot(x, w3_ref[0])
    return _dot(hidden.astype(BF16), w2_ref[0])


def _ple(h2, p, pg, wg, wp):
    gate = jax.nn.sigmoid(_dot(_rms(h2, pg).astype(BF16), wg))
    return h2 + gate * _dot(p.astype(BF16), wp)


def _ffn_kernel(f_ref, w1_ref, w3_ref, w2_ref, h1_ref, p_ref, pg_ref, wg_ref, wp_ref,
                o_ref, acc_ref):
    c = pl.program_id(1)

    @pl.when(c == 0)
    def _():
        acc_ref[...] = jnp.zeros_like(acc_ref)

    acc_ref[...] += _swiglu_step(f_ref[...], w1_ref, w3_ref, w2_ref)

    @pl.when(c == pl.num_programs(1) - 1)
    def _():
        o_ref[...] = _ple(h1_ref[...] + acc_ref[...], p_ref[0, 0], pg_ref[...], wg_ref[...],
                          wp_ref[...])


def _layer_rows(p, layer, tm):
    tiles_per_seq = p.shape[2] // tm
    return ((1, 1, tm, p.shape[3]),
            lambda i, *_: (layer, i // tiles_per_seq, i % tiles_per_seq, 0))


def _ffn(f, w1, w3, w2, h1, p, layer, pg, wg, wp, *, tm, tf):
    n, d = h1.shape
    d_ff = w1.shape[2]
    row = lambda i, c: (i, 0)
    fixed = lambda i, c: (0, 0)
    once = dict(pipeline_mode=pl.Buffered(1))
    w_once = once if tf == d_ff else {}
    in_specs = [
        pl.BlockSpec((tm, d), row),
        pl.BlockSpec((1, d, tf), lambda i, c: (0, 0, c), **w_once),
        pl.BlockSpec((1, d, tf), lambda i, c: (0, 0, c), **w_once),
        pl.BlockSpec((1, tf, d), lambda i, c: (0, c, 0), **w_once),
        pl.BlockSpec((tm, d), row),
        pl.BlockSpec(*_layer_rows(p, layer, tm)),
        pl.BlockSpec((1, d), fixed, **once),
        pl.BlockSpec(wg.shape, fixed, **once),
        pl.BlockSpec(wp.shape, fixed, **once),
    ]
    return pl.pallas_call(
        _ffn_kernel,
        grid=(n // tm, d_ff // tf),
        in_specs=in_specs,
        out_specs=pl.BlockSpec((tm, d), row),
        out_shape=jax.ShapeDtypeStruct((n, d), F32),
        scratch_shapes=[pltpu.VMEM((tm, d), F32)],
        compiler_params=pltpu.CompilerParams(
            dimension_semantics=("arbitrary", "arbitrary"),
            vmem_limit_bytes=VMEM_LIMIT_BYTES),
        name="ffn_dense",
    )(f, w1, w3, w2, h1, p, pg, wg, wp)


def _route_plan(route, n_experts, tm):
    n = route.shape[1]
    pairs = TOP_K * n
    n_rows = pairs + n_experts * tm
    eid = route[0:TOP_K].astype(jnp.int32).reshape(pairs)
    gates = route[TOP_K:2 * TOP_K].reshape(pairs)
    pair_bits = int(np.ceil(np.log2(pairs)))
    keys = jnp.sort(eid * (1 << pair_bits) + jnp.arange(pairs, dtype=jnp.int32))
    order = keys & ((1 << pair_bits) - 1)
    bounds = jnp.arange(n_experts + 1, dtype=jnp.int32) << pair_bits
    bounds = bounds.at[n_experts].set(jnp.iinfo(jnp.int32).max)
    first = jnp.sum(keys[None, :] < bounds[:, None], axis=1).astype(jnp.int32)
    counts = first[1:] - first[:-1]
    padded = (counts + tm - 1) // tm * tm
    ends = jnp.cumsum(padded)
    n_tiles = n_rows // tm
    tile_start = jnp.arange(n_tiles, dtype=jnp.int32) * tm
    tile_expert = jnp.sum(tile_start[:, None] >= ends[None, :], axis=1)
    tile_expert = jnp.minimum(tile_expert, n_experts - 1).astype(jnp.int32)
    in_group = tile_start - (ends - padded)[tile_expert]
    tile_rows = jnp.clip(counts[tile_expert] - in_group, 0, tm)
    lane = jnp.arange(tm, dtype=jnp.int32)[None, :]
    real = lane < tile_rows[:, None]
    safe = order[jnp.clip((first[tile_expert] + in_group)[:, None] + lane, 0, pairs - 1)]
    src_token = jnp.where(real, safe % n, 0)
    gate_row = jnp.where(real, gates[safe], 0.0)
    spare = pairs + (jnp.arange(n_tiles, dtype=jnp.int32) % 2 * tm)[:, None] + lane
    dst_row = jnp.where(real, safe, spare)
    src_token = jnp.concatenate([src_token, jnp.zeros((1, tm), jnp.int32)])
    dst_row = jnp.concatenate([pairs + tm + lane, dst_row])
    return (src_token.reshape(n_tiles + 1, 1, tm), dst_row.reshape(n_tiles + 1, 1, tm),
            jnp.broadcast_to(gate_row[:, None, :], (n_tiles, SUBLANES, tm)), tile_expert,
            tile_rows)


def _for_each_row(count, fn, unroll=8):
    assert count % unroll == 0

    def chunk(i, carry):
        for u in range(unroll):
            fn(i * unroll + u)
        return carry

    lax.fori_loop(0, count // unroll, chunk, 0)


def _moe_ffn_kernel(expert_ref, count_ref, src0_ref, src_ref, dst_ref, f_hbm, gate_ref,
                    w1_ref, w3_ref, w2_ref, y_hbm, rows_ref, x_ref, acc_ref, stage_ref,
                    gather_sem, scatter_sem, *, tm, n_chunks, n_pairs):
    j = pl.program_id(0)
    c = pl.program_id(1)
    last_tile = pl.num_programs(0) - 1
    last_chunk = n_chunks - 1
    per_step = tm // n_chunks
    active = count_ref[j] > 0
    cur = j % 2
    other = 1 - cur

    def tile_gather(slot):
        return pltpu.make_async_copy(f_hbm.at[pl.ds(0, tm), :], rows_ref.at[slot],
                                     gather_sem.at[slot])

    def tile_scatter(slot):
        return pltpu.make_async_copy(stage_ref.at[slot], y_hbm.at[pl.ds(0, tm), :],
                                     scatter_sem.at[slot])

    @pl.when((j == 0) & (c == 0))
    def _():
        stage_ref[...] = jnp.zeros_like(stage_ref)
        for slot in range(2):
            spare = pltpu.make_async_copy(stage_ref.at[slot],
                                          y_hbm.at[pl.ds(n_pairs + slot * tm, tm), :],
                                          scatter_sem.at[slot])
            spare.start()
            spare.wait()

        def gather_first(r):
            pltpu.make_async_copy(f_hbm.at[pl.ds(src0_ref[0, 0, r], 1), :],
                                  rows_ref.at[0, pl.ds(r, 1), :], gather_sem.at[0]).start()
        _for_each_row(tm, gather_first)

    @pl.when(c == 0)
    def _():
        tile_gather(cur).wait()

        @pl.when(j >= 1)
        def _():
            tile_scatter(cur).wait()

        x_ref[...] = rows_ref[cur].astype(BF16)
        acc_ref[...] = jnp.zeros_like(acc_ref)

    def start_row_copies():
        base = pl.multiple_of(c * per_step, per_step)
        for r in range(per_step):
            pltpu.make_async_copy(f_hbm.at[pl.ds(src_ref[0, 0, base + r], 1), :],
                                  rows_ref.at[other, pl.ds(base + r, 1), :],
                                  gather_sem.at[other]).start()
            pltpu.make_async_copy(stage_ref.at[other, pl.ds(base + r, 1), :],
                                  y_hbm.at[pl.ds(dst_ref[0, 0, base + r], 1), :],
                                  scatter_sem.at[other]).start()

    @pl.when(active)
    def _():
        start_row_copies()
        acc_ref[...] += _swiglu_step(x_ref[...], w1_ref, w3_ref, w2_ref)

    @pl.when(jnp.logical_not(active))
    def _():
        start_row_copies()

    @pl.when(active & (c == last_chunk))
    def _():
        gate = jnp.transpose(gate_ref[0])[:, 0:1]
        stage_ref[cur] = acc_ref[...] * gate

    @pl.when((j == last_tile) & (c == last_chunk))
    def _():
        tile_gather(other).wait()
        tile_scatter(other).wait()


def _moe_ffn(f, plan, w1, w3, w2, *, tm, tf):
    src_token, dst_row, gate_row, tile_expert, tile_rows = plan
    n, d = f.shape
    n_exp, _, d_ff = w1.shape
    n_tiles = tile_rows.shape[0]
    n_chunks = d_ff // tf
    n_pairs = TOP_K * n
    assert n_pairs % tm == 0 and tm % n_chunks == 0 and n_exp < tm
    chunk = lambda j, c, count: jnp.where(count[j] > 0, c, n_chunks - 1)
    grid_spec = pltpu.PrefetchScalarGridSpec(
        num_scalar_prefetch=2,
        grid=(n_tiles, n_chunks),
        in_specs=[
            pl.BlockSpec((1, 1, tm), lambda j, c, ex, count: (0, 0, 0), memory_space=pltpu.SMEM),
            pl.BlockSpec((1, 1, tm), lambda j, c, ex, count: (j + 1, 0, 0),
                         memory_space=pltpu.SMEM),
            pl.BlockSpec((1, 1, tm), lambda j, c, ex, count: (j, 0, 0), memory_space=pltpu.SMEM),
            pl.BlockSpec(memory_space=pl.ANY),
            pl.BlockSpec((1, SUBLANES, tm), lambda j, c, ex, count: (j, 0, 0)),
            pl.BlockSpec((1, d, tf), lambda j, c, ex, count: (ex[j], 0, chunk(j, c, count))),
            pl.BlockSpec((1, d, tf), lambda j, c, ex, count: (ex[j], 0, chunk(j, c, count))),
            pl.BlockSpec((1, tf, d), lambda j, c, ex, count: (ex[j], chunk(j, c, count), 0)),
        ],
        out_specs=pl.BlockSpec(memory_space=pl.ANY),
        scratch_shapes=[pltpu.VMEM((2, tm, d), F32), pltpu.VMEM((tm, d), BF16),
                        pltpu.VMEM((tm, d), F32), pltpu.VMEM((2, tm, d), F32),
                        pltpu.SemaphoreType.DMA((2,)), pltpu.SemaphoreType.DMA((2,))],
    )
    return pl.pallas_call(
        functools.partial(_moe_ffn_kernel, tm=tm, n_chunks=n_chunks, n_pairs=n_pairs),
        grid_spec=grid_spec,
        out_shape=jax.ShapeDtypeStruct((n_pairs + 2 * tm, d), F32),
        compiler_params=pltpu.CompilerParams(
            dimension_semantics=("arbitrary", "arbitrary"),
            vmem_limit_bytes=VMEM_LIMIT_BYTES),
        name="ffn_moe",
    )(tile_expert, tile_rows, src_token, src_token, dst_row, f, gate_row, w1, w3, w2)


def _ple_kernel(h1_ref, y0_ref, y1_ref, p_ref, pg_ref, wg_ref, wp_ref, o_ref):
    h2 = h1_ref[...] + y0_ref[...] + y1_ref[...]
    o_ref[...] = _ple(h2, p_ref[0, 0], pg_ref[...], wg_ref[...], wp_ref[...])


def _ple_combine(h1, y, p, layer, pg, wg, wp, *, tm):
    n, d = h1.shape
    row = lambda i: (i, 0)
    fixed = lambda i: (0, 0)
    return pl.pallas_call(
        _ple_kernel,
        grid=(n // tm,),
        in_specs=[
            pl.BlockSpec((tm, d), row),
            pl.BlockSpec((tm, d), row),
            pl.BlockSpec((tm, d), lambda i: (i + n // tm, 0)),
            pl.BlockSpec(*_layer_rows(p, layer, tm)),
            pl.BlockSpec((1, d), fixed),
            pl.BlockSpec(wg.shape, fixed),
            pl.BlockSpec(wp.shape, fixed),
        ],
        out_specs=pl.BlockSpec((tm, d), row),
        out_shape=jax.ShapeDtypeStruct((n, d), F32),
        compiler_params=pltpu.CompilerParams(
            dimension_semantics=("arbitrary",), vmem_limit_bytes=VMEM_LIMIT_BYTES),
        name="ple_combine",
    )(h1, y, y, p, pg, wg, wp)


def kernel(x, p, mix_norm_g, w_in, q_norm_g, k_norm_g, conv_w, conv_b, attn_out_g, conv_out_g,
           w_o, ffn_norm_g, dense_w1, dense_w3, dense_w2, router_w, moe_w1, moe_w3, moe_w2,
           ple_norm_g, ple_gate_w, ple_proj_w):
    bsz, seq_len, d = x.shape
    depth = w_in.shape[0]
    sb = attn_out_g.shape[1]
    n_heads = sb // HEAD_DIM
    n_experts = router_w.shape[2]
    n = bsz * seq_len
    tm = min(ROW_TILE, seq_len)
    t = min(ATTN_TILE, seq_len)
    tm_ffn = min(FFN_ROW_TILE, seq_len)

    head_id = jnp.arange(sb) // HEAD_DIM
    head_mean = jnp.where(head_id[:, None] == head_id[None, :], 1.0 / HEAD_DIM, 0.0).astype(BF16)
    ar = jnp.arange(t)
    tri = (ar[:, None] >= ar[None, :]).astype(BF16)
    q_scale = LOG2E / float(np.sqrt(HEAD_DIM))
    row2 = lambda a: a.reshape(1, -1)

    h = x.reshape(n, d)
    for i in range(depth):
        qg = row2(jnp.tile(q_norm_g[i], n_heads) * q_scale)
        kg = row2(jnp.tile(k_norm_g[i], n_heads))
        q, k, v, cu, b = _mix_in(h, row2(mix_norm_g[i]), w_in[i].astype(BF16), qg, kg,
                                 head_mean, tm=tm, seq_len=seq_len)
        attn = _attention(q, k, v, tri, t=t, pairs=ATTN_PAIRS)
        j = i // 2
        routed = i % 2 == 1
        rw = None
        if routed:
            e_pad = -(-n_experts // SUBLANES) * SUBLANES
            rw = jnp.pad(router_w[j].T, ((0, e_pad - n_experts), (0, 0)))
            rw_hi = rw.astype(BF16)
            rw = jnp.stack([rw_hi, (rw - rw_hi.astype(F32)).astype(BF16)])
        outs = _mix_out(attn, cu, b, h, row2(attn_out_g[i]), row2(conv_out_g[i]), conv_w[i],
                        row2(conv_b[i]), w_o[i].astype(BF16), row2(ffn_norm_g[i]), rw,
                        tm=tm, seq_len=seq_len, n_experts=n_experts if routed else 0)
        ple_w = (row2(ple_norm_g[i]), ple_gate_w[i].astype(BF16), ple_proj_w[i].astype(BF16))
        if routed:
            h1, f, route = outs
            plan = _route_plan(route, n_experts, MOE_ROW_TILE)
            y = _moe_ffn(f, plan, moe_w1[j].astype(BF16), moe_w3[j].astype(BF16),
                         moe_w2[j].astype(BF16), tm=MOE_ROW_TILE, tf=MOE_COL_TILE)
            h = _ple_combine(h1, y, p, i, *ple_w, tm=tm)
        else:
            h1, f = outs
            h = _ffn(f, dense_w1[j:j + 1].astype(BF16), dense_w3[j:j + 1].astype(BF16),
                     dense_w2[j:j + 1].astype(BF16), h1, p, i, *ple_w, tm=tm_ffn,
                     tf=FFN_COL_TILE)
    return h.reshape(bsz, seq_len, d)
```

```python
import functools

import numpy as np
import jax
import jax.numpy as jnp
from jax import lax
from jax.experimental import pallas as pl
from jax.experimental.pallas import tpu as pltpu

F32 = jnp.float32
BF16 = jnp.bfloat16

EPS = 1e-6
HEAD_DIM = 64
CONV_K = 3
TOP_K = 2
LANES = 128
SUBLANES = 8
HALO_ROWS = 2 * SUBLANES
LOG2E = float(np.log2(np.e))

VMEM_LIMIT_BYTES = 56 * 1024 * 1024

ROW_TILE = 1024
ATTN_TILE = 256
ATTN_PAIRS = 4
SP_CLAMP = 64.0
DONE_LOG2 = 160.0
NO_BLOCK_DECAY = 1e30
FFN_ROW_TILE = 512
FFN_COL_TILE = 3584
MOE_ROW_TILE = 512
MOE_COL_TILE = 3584


def _rms(x, g):
    return x * lax.rsqrt(jnp.mean(x * x, axis=-1, keepdims=True) + EPS) * g


def _dot(a, b):
    return jnp.dot(a, b, preferred_element_type=F32)


def _mix_in_kernel(h_ref, g_ref, w_ref, qg_ref, kg_ref, hm_ref,
                   q_ref, k_ref, v_ref, cu_ref, b_ref, *, sb):
    a = _rms(h_ref[...], g_ref[...]).astype(BF16)
    hm = hm_ref[...]

    def qk_norm(t, g):
        ms = _dot((t * t).astype(BF16), hm)
        return (t * lax.rsqrt(ms + EPS) * g).astype(BF16)

    q_ref[0] = qk_norm(_dot(a, w_ref[:, 0:sb]), qg_ref[...])
    k_ref[0] = qk_norm(_dot(a, w_ref[:, sb:2 * sb]), kg_ref[...])
    v_ref[0] = _dot(a, w_ref[:, 2 * sb:3 * sb]).astype(BF16)
    u = _dot(a, w_ref[:, 3 * sb:4 * sb])
    c = _dot(a, w_ref[:, 4 * sb:5 * sb])
    cu_ref[...] = (c * u).astype(BF16)
    b_ref[...] = _dot(a, w_ref[:, 5 * sb:6 * sb]).astype(BF16)


def _mix_in(h, g, w_in, qg, kg, head_mean, *, tm, seq_len):
    n, d = h.shape
    sb = head_mean.shape[0]
    tiles_per_seq = seq_len // tm
    row = lambda i: (i, 0)
    fixed = lambda i: (0, 0)
    seq_row = lambda i: (i // tiles_per_seq, i % tiles_per_seq, 0)
    out_bf = jax.ShapeDtypeStruct((n // seq_len, seq_len, sb), BF16)
    out_rows = jax.ShapeDtypeStruct((n, sb), BF16)
    return pl.pallas_call(
        functools.partial(_mix_in_kernel, sb=sb),
        grid=(n // tm,),
        in_specs=[
            pl.BlockSpec((tm, d), row),
            pl.BlockSpec((1, d), fixed),
            pl.BlockSpec(w_in.shape, fixed),
            pl.BlockSpec((1, sb), fixed),
            pl.BlockSpec((1, sb), fixed),
            pl.BlockSpec((sb, sb), fixed),
        ],
        out_specs=[pl.BlockSpec((1, tm, sb), seq_row)] * 3 + [pl.BlockSpec((tm, sb), row)] * 2,
        out_shape=[out_bf, out_bf, out_bf, out_rows, out_rows],
        compiler_params=pltpu.CompilerParams(
            dimension_semantics=("arbitrary",), vmem_limit_bytes=VMEM_LIMIT_BYTES),
        name="mix_in",
    )(h, g, w_in, qg, kg, head_mean)


def _attn_kernel(q_ref, k_ref, v_ref, u_ref, o_ref, acc_ref, c_ref, *, t, pairs):
    i = pl.program_id(2)
    heads = 2 * pairs
    q = q_ref[0]
    lane = lax.broadcasted_iota(jnp.int32, (t, LANES), 1)
    lo_half = lane < HEAD_DIM
    q_heads = []
    for p in range(pairs):
        qp = q[:, p * LANES:(p + 1) * LANES]
        q_heads += [jnp.where(lo_half, qp, jnp.zeros_like(qp)),
                    jnp.where(lo_half, jnp.zeros_like(qp), qp)]
    tri = u_ref[...]
    row = lax.broadcasted_iota(jnp.int32, (t, t), 0)
    col = lax.broadcasted_iota(jnp.int32, (t, t), 1)
    causal = col < row

    acc_ref[...] = jnp.zeros_like(acc_ref)
    c_ref[...] = jnp.zeros_like(c_ref)

    def block(j, diagonal, extra_decay=0.0):
        start = pl.multiple_of(j * t, t)
        kj = k_ref[0, pl.ds(start, t), :]
        vj = v_ref[0, pl.ds(start, t), :]
        for hd in range(heads):
            p = hd // 2
            z = lax.dot_general(q_heads[hd], kj[:, p * LANES:(p + 1) * LANES],
                                (((1,), (1,)), ((), ())), preferred_element_type=F32)
            sp = jnp.maximum(z, jnp.log2(1.0 + jnp.exp2(jnp.minimum(z, SP_CLAMP))))
            if diagonal:
                sp = jnp.where(causal, sp, 0.0)
            incl = _dot(sp.astype(BF16), tri)
            w = jnp.exp2(z - incl - (c_ref[hd] + extra_decay))
            if diagonal:
                w = jnp.where(causal, w, 0.0)
            acc_ref[hd] += _dot(w.astype(BF16), vj[:, p * LANES:(p + 1) * LANES])
            c_ref[hd] += incl[:, 0:1]

    def decay_floor():
        m = jnp.min(c_ref[0])
        for hd in range(1, heads):
            m = jnp.minimum(m, jnp.min(c_ref[hd]))
        return m

    block(i, True)
    block(jnp.maximum(i - 1, 0), False, extra_decay=jnp.where(i > 0, 0.0, NO_BLOCK_DECAY))

    def cond(carry):
        j, floor = carry
        return (j >= 0) & (floor < DONE_LOG2)

    def body(carry):
        j, _ = carry
        block(j, False)
        return j - 1, decay_floor()

    lax.while_loop(cond, body, (i - 2, decay_floor()))
    for p in range(pairs):
        o_ref[0, :, p * LANES:(p + 1) * LANES] = jnp.where(
            lo_half, acc_ref[2 * p], acc_ref[2 * p + 1]).astype(o_ref.dtype)


def _attention(q, k, v, tri, *, t, pairs):
    bsz, s, sb = q.shape
    width = LANES * pairs
    blk = lambda b, g, i: (b, i, g)
    seq = lambda b, g, i: (b, 0, g)
    return pl.pallas_call(
        functools.partial(_attn_kernel, t=t, pairs=pairs),
        grid=(bsz, sb // width, s // t),
        in_specs=[
            pl.BlockSpec((1, t, width), blk),
            pl.BlockSpec((1, s, width), seq),
            pl.BlockSpec((1, s, width), seq),
            pl.BlockSpec((t, t), lambda b, g, i: (0, 0)),
        ],
        out_specs=pl.BlockSpec((1, t, width), blk),
        out_shape=jax.ShapeDtypeStruct((bsz, s, sb), BF16),
        scratch_shapes=[pltpu.VMEM((2 * pairs, t, LANES), F32),
                        pltpu.VMEM((2 * pairs, t, 1), F32)],
        compiler_params=pltpu.CompilerParams(
            dimension_semantics=("arbitrary", "arbitrary", "arbitrary"),
            vmem_limit_bytes=VMEM_LIMIT_BYTES),
        name="attention",
    )(q, k, v, tri)


def _mix_out_kernel(*refs, tiles_per_seq, sb, n_experts):
    (attn_ref, cu_ref, halo_ref, b_ref, h_ref, ag_ref, cg_ref, cw_ref, cb_ref,
     wo_ref, fg_ref) = refs[:11]
    if n_experts:
        rw_ref, h1_ref, f_ref, route_ref = refs[11:]
    else:
        h1_ref, f_ref = refs[11:]

    attn_n = _rms(attn_ref[0].astype(F32), ag_ref[...])

    cu = cu_ref[...].astype(F32)
    first = pl.program_id(0) % tiles_per_seq == 0
    halo = jnp.where(first, 0.0, halo_ref[...].astype(F32))
    rowi = lax.broadcasted_iota(jnp.int32, cu.shape, 0)
    prev1 = jnp.where(rowi == 0, halo[HALO_ROWS - 1:HALO_ROWS], pltpu.roll(cu, 1, axis=0))
    prev2 = jnp.where(rowi == 0, halo[HALO_ROWS - 2:HALO_ROWS - 1],
                      jnp.where(rowi == 1, halo[HALO_ROWS - 1:HALO_ROWS],
                                pltpu.roll(cu, 2, axis=0)))
    conv = cw_ref[0:1] * prev2 + cw_ref[1:2] * prev1 + cw_ref[2:3] * cu + cb_ref[...]
    conv_n = _rms(b_ref[...].astype(F32) * conv, cg_ref[...])

    h1 = (h_ref[...] + _dot(attn_n.astype(BF16), wo_ref[0:sb])
          + _dot(conv_n.astype(BF16), wo_ref[sb:]))
    h1_ref[...] = h1
    f = _rms(h1, fg_ref[...])
    f_ref[...] = f.astype(f_ref.dtype)

    if n_experts:
        f_hi = f.astype(BF16)
        f_lo = (f - f_hi.astype(F32)).astype(BF16)
        nt = (((1,), (1,)), ((), ()))
        logits = (lax.dot_general(rw_ref[0], f_hi, nt, preferred_element_type=F32)
                  + (lax.dot_general(rw_ref[0], f_lo, nt, preferred_element_type=F32)
                     + lax.dot_general(rw_ref[1], f_hi, nt, preferred_element_type=F32)))
        e_pad = logits.shape[0]
        eid = lax.broadcasted_iota(jnp.int32, logits.shape, 0).astype(F32)
        neg = jnp.float32(-jnp.inf)
        logits = jnp.where(eid < n_experts, logits, neg)
        m1 = jnp.max(logits, axis=0, keepdims=True)
        i1 = jnp.min(jnp.where(logits == m1, eid, float(e_pad)), axis=0, keepdims=True)
        rest = jnp.where(eid == i1, neg, logits)
        m2 = jnp.max(rest, axis=0, keepdims=True)
        i2 = jnp.min(jnp.where(rest == m2, eid, float(e_pad)), axis=0, keepdims=True)
        e2 = jnp.exp(m2 - m1)
        g1 = 1.0 / (1.0 + e2)
        g2 = e2 * g1
        slot = lax.broadcasted_iota(jnp.int32, route_ref.shape, 0)
        route_ref[...] = jnp.where(
            slot == 0, i1, jnp.where(slot == 1, i2, jnp.where(
                slot == 2, g1, jnp.where(slot == 3, g2, 0.0))))


def _mix_out(attn, cu, b, h, ag, cg, cw, cb, w_o, fg, router_w, *, tm, seq_len, n_experts):
    n, d = h.shape
    sb = attn.shape[2]
    tiles_per_seq = seq_len // tm
    row = lambda i: (i, 0)
    fixed = lambda i: (0, 0)
    halo = lambda i: (jnp.maximum(i * (tm // HALO_ROWS) - 1, 0), 0)
    in_specs = [
        pl.BlockSpec((1, tm, sb), lambda i: (i // tiles_per_seq, i % tiles_per_seq, 0)),
        pl.BlockSpec((tm, sb), row),
        pl.BlockSpec((HALO_ROWS, sb), halo),
        pl.BlockSpec((tm, sb), row),
        pl.BlockSpec((tm, d), row),
        pl.BlockSpec((1, sb), fixed),
        pl.BlockSpec((1, sb), fixed),
        pl.BlockSpec((CONV_K, sb), fixed),
        pl.BlockSpec((1, sb), fixed),
        pl.BlockSpec(w_o.shape, fixed),
        pl.BlockSpec((1, d), fixed),
    ]
    args = [attn, cu, cu, b, h, ag, cg, cw, cb, w_o, fg]
    out_specs = [pl.BlockSpec((tm, d), row), pl.BlockSpec((tm, d), row)]
    out_shape = [jax.ShapeDtypeStruct((n, d), F32),
                 jax.ShapeDtypeStruct((n, d), F32 if n_experts else BF16)]
    if n_experts:
        in_specs.append(pl.BlockSpec(router_w.shape, lambda i: (0, 0, 0)))
        args.append(router_w)
        out_specs.append(pl.BlockSpec((SUBLANES, tm), lambda i: (0, i)))
        out_shape.append(jax.ShapeDtypeStruct((SUBLANES, n), F32))
    return pl.pallas_call(
        functools.partial(_mix_out_kernel, tiles_per_seq=seq_len // tm, sb=sb,
                          n_experts=n_experts),
        grid=(n // tm,),
        in_specs=in_specs,
        out_specs=out_specs,
        out_shape=out_shape,
        compiler_params=pltpu.CompilerParams(
            dimension_semantics=("arbitrary",), vmem_limit_bytes=VMEM_LIMIT_BYTES),
        name="mix_out_moe" if n_experts else "mix_out",
    )(*args)


def _swiglu_step(x, w1_ref, w3_ref, w2_ref):
    a = _dot(x, w1_ref[0])
    hidden = a * jax.nn.sigmoid(a) * _dot(x, w3_ref[0])
    return _dot(hidden.astype(BF16), w2_ref[0])


def _ple(h2, p, pg, wg, wp):
    gate = jax.nn.sigmoid(_dot(_rms(h2, pg).astype(BF16), wg))
    return h2 + gate * _dot(p.astype(BF16), wp)


def _ffn_kernel(f_ref, w1_ref, w3_ref, w2_ref, h1_ref, p_ref, pg_ref, wg_ref, wp_ref,
                o_ref, acc_ref):
    c = pl.program_id(1)

    @pl.when(c == 0)
    def _():
        acc_ref[...] = jnp.zeros_like(acc_ref)

    acc_ref[...] += _swiglu_step(f_ref[...], w1_ref, w3_ref, w2_ref)

    @pl.when(c == pl.num_programs(1) - 1)
    def _():
        o_ref[...] = _ple(h1_ref[...] + acc_ref[...], p_ref[0, 0], pg_ref[...], wg_ref[...],
                          wp_ref[...])


def _layer_rows(p, layer, tm):
    tiles_per_seq = p.shape[2] // tm
    return ((1, 1, tm, p.shape[3]),
            lambda i, *_: (layer, i // tiles_per_seq, i % tiles_per_seq, 0))


def _ffn(f, w1, w3, w2, h1, p, layer, pg, wg, wp, *, tm, tf):
    n, d = h1.shape
    d_ff = w1.shape[2]
    row = lambda i, c: (i, 0)
    fixed = lambda i, c: (0, 0)
    once = dict(pipeline_mode=pl.Buffered(1))
    w_once = once if tf == d_ff else {}
    in_specs = [
        pl.BlockSpec((tm, d), row),
        pl.BlockSpec((1, d, tf), lambda i, c: (0, 0, c), **w_once),
        pl.BlockSpec((1, d, tf), lambda i, c: (0, 0, c), **w_once),
        pl.BlockSpec((1, tf, d), lambda i, c: (0, c, 0), **w_once),
        pl.BlockSpec((tm, d), row),
        pl.BlockSpec(*_layer_rows(p, layer, tm)),
        pl.BlockSpec((1, d), fixed, **once),
        pl.BlockSpec(wg.shape, fixed, **once),
        pl.BlockSpec(wp.shape, fixed, **once),
    ]
    return pl.pallas_call(
        _ffn_kernel,
        grid=(n // tm, d_ff // tf),
        in_specs=in_specs,
        out_specs=pl.BlockSpec((tm, d), row),
        out_shape=jax.ShapeDtypeStruct((n, d), F32),
        scratch_shapes=[pltpu.VMEM((tm, d), F32)],
        compiler_params=pltpu.CompilerParams(
            dimension_semantics=("arbitrary", "arbitrary"),
            vmem_limit_bytes=VMEM_LIMIT_BYTES),
        name="ffn_dense",
    )(f, w1, w3, w2, h1, p, pg, wg, wp)


def _route_plan(route, n_experts, tm):
    n = route.shape[1]
    pairs = TOP_K * n
    n_rows = pairs + n_experts * tm
    eid = route[0:TOP_K].astype(jnp.int32).reshape(pairs)
    gates = route[TOP_K:2 * TOP_K].reshape(pairs)
    pair_bits = int(np.ceil(np.log2(pairs)))
    keys = jnp.sort(eid * (1 << pair_bits) + jnp.arange(pairs, dtype=jnp.int32))
    order = keys & ((1 << pair_bits) - 1)
    bounds = jnp.arange(n_experts + 1, dtype=jnp.int32) << pair_bits
    bounds = bounds.at[n_experts].set(jnp.iinfo(jnp.int32).max)
    first = jnp.sum(keys[None, :] < bounds[:, None], axis=1).astype(jnp.int32)
    counts = first[1:] - first[:-1]
    padded = (counts + tm - 1) // tm * tm
    ends = jnp.cumsum(padded)
    n_tiles = n_rows // tm
    tile_start = jnp.arange(n_tiles, dtype=jnp.int32) * tm
    tile_expert = jnp.sum(tile_start[:, None] >= ends[None, :], axis=1)
    tile_expert = jnp.minimum(tile_expert, n_experts - 1).astype(jnp.int32)
    in_group = tile_start - (ends - padded)[tile_expert]
    tile_rows = jnp.clip(counts[tile_expert] - in_group, 0, tm)
    lane = jnp.arange(tm, dtype=jnp.int32)[None, :]
    real = lane < tile_rows[:, None]
    safe = order[jnp.clip((first[tile_expert] + in_group)[:, None] + lane, 0, pairs - 1)]
    src_token = jnp.where(real, safe % n, 0)
    gate_row = jnp.where(real, gates[safe], 0.0)
    spare = pairs + (jnp.arange(n_tiles, dtype=jnp.int32) % 2 * tm)[:, None] + lane
    dst_row = jnp.where(real, safe, spare)
    src_token = jnp.concatenate([src_token, jnp.zeros((1, tm), jnp.int32)])
    dst_row = jnp.concatenate([pairs + tm + lane, dst_row])
    return (src_token.reshape(n_tiles + 1, 1, tm), dst_row.reshape(n_tiles + 1, 1, tm),
            jnp.broadcast_to(gate_row[:, None, :], (n_tiles, SUBLANES, tm)), tile_expert,
            tile_rows)


def _for_each_row(count, fn, unroll=8):
    assert count % unroll == 0

    def chunk(i, carry):
        for u in range(unroll):
            fn(i * unroll + u)
        return carry

    lax.fori_loop(0, count // unroll, chunk, 0)


def _moe_ffn_kernel(expert_ref, count_ref, src0_ref, src_ref, dst_ref, f_hbm, gate_ref,
                    w1_ref, w3_ref, w2_ref, y_hbm, rows_ref, x_ref, acc_ref, stage_ref,
                    gather_sem, scatter_sem, *, tm, n_chunks, n_pairs):
    j = pl.program_id(0)
    c = pl.program_id(1)
    last_tile = pl.num_programs(0) - 1
    last_chunk = n_chunks - 1
    per_step = tm // n_chunks
    active = count_ref[j] > 0
    cur = j % 2
    other = 1 - cur

    def tile_gather(slot):
        return pltpu.make_async_copy(f_hbm.at[pl.ds(0, tm), :], rows_ref.at[slot],
                                     gather_sem.at[slot])

    def tile_scatter(slot):
        return pltpu.make_async_copy(stage_ref.at[slot], y_hbm.at[pl.ds(0, tm), :],
                                     scatter_sem.at[slot])

    @pl.when((j == 0) & (c == 0))
    def _():
        stage_ref[...] = jnp.zeros_like(stage_ref)
        for slot in range(2):
            spare = pltpu.make_async_copy(stage_ref.at[slot],
                                          y_hbm.at[pl.ds(n_pairs + slot * tm, tm), :],
                                          scatter_sem.at[slot])
            spare.start()
            spare.wait()

        def gather_first(r):
            pltpu.make_async_copy(f_hbm.at[pl.ds(src0_ref[0, 0, r], 1), :],
                                  rows_ref.at[0, pl.ds(r, 1), :], gather_sem.at[0]).start()
        _for_each_row(tm, gather_first)

    @pl.when(c == 0)
    def _():
        tile_gather(cur).wait()

        @pl.when(j >= 1)
        def _():
            tile_scatter(cur).wait()

        x_ref[...] = rows_ref[cur].astype(BF16)
        acc_ref[...] = jnp.zeros_like(acc_ref)

    def start_row_copies():
        base = pl.multiple_of(c * per_step, per_step)
        for r in range(per_step):
            pltpu.make_async_copy(f_hbm.at[pl.ds(src_ref[0, 0, base + r], 1), :],
                                  rows_ref.at[other, pl.ds(base + r, 1), :],
                                  gather_sem.at[other]).start()
            pltpu.make_async_copy(stage_ref.at[other, pl.ds(base + r, 1), :],
                                  y_hbm.at[pl.ds(dst_ref[0, 0, base + r], 1), :],
                                  scatter_sem.at[other]).start()

    @pl.when(active)
    def _():
        start_row_copies()
        acc_ref[...] += _swiglu_step(x_ref[...], w1_ref, w3_ref, w2_ref)

    @pl.when(jnp.logical_not(active))
    def _():
        start_row_copies()

    @pl.when(active & (c == last_chunk))
    def _():
        gate = jnp.transpose(gate_ref[0])[:, 0:1]
        stage_ref[cur] = acc_ref[...] * gate

    @pl.when((j == last_tile) & (c == last_chunk))
    def _():
        tile_gather(other).wait()
        tile_scatter(other).wait()


def _moe_ffn(f, plan, w1, w3, w2, *, tm, tf):
    src_token, dst_row, gate_row, tile_expert, tile_rows = plan
    n, d = f.shape
    n_exp, _, d_ff = w1.shape
    n_tiles = tile_rows.shape[0]
    n_chunks = d_ff // tf
    n_pairs = TOP_K * n
    assert n_pairs % tm == 0 and tm % n_chunks == 0 and n_exp < tm
    chunk = lambda j, c, count: jnp.where(count[j] > 0, c, n_chunks - 1)
    w_once = dict(pipeline_mode=pl.Buffered(1)) if tf == d_ff else {}
    grid_spec = pltpu.PrefetchScalarGridSpec(
        num_scalar_prefetch=2,
        grid=(n_tiles, n_chunks),
        in_specs=[
            pl.BlockSpec((1, 1, tm), lambda j, c, ex, count: (0, 0, 0), memory_space=pltpu.SMEM),
            pl.BlockSpec((1, 1, tm), lambda j, c, ex, count: (j + 1, 0, 0),
                         memory_space=pltpu.SMEM),
            pl.BlockSpec((1, 1, tm), lambda j, c, ex, count: (j, 0, 0), memory_space=pltpu.SMEM),
            pl.BlockSpec(memory_space=pl.ANY),
            pl.BlockSpec((1, SUBLANES, tm), lambda j, c, ex, count: (j, 0, 0)),
            pl.BlockSpec((1, d, tf), lambda j, c, ex, count: (ex[j], 0, chunk(j, c, count)),
                         **w_once),
            pl.BlockSpec((1, d, tf), lambda j, c, ex, count: (ex[j], 0, chunk(j, c, count)),
                         **w_once),
            pl.BlockSpec((1, tf, d), lambda j, c, ex, count: (ex[j], chunk(j, c, count), 0),
                         **w_once),
        ],
        out_specs=pl.BlockSpec(memory_space=pl.ANY),
        scratch_shapes=[pltpu.VMEM((2, tm, d), F32), pltpu.VMEM((tm, d), BF16),
                        pltpu.VMEM((tm, d), F32), pltpu.VMEM((2, tm, d), F32),
                        pltpu.SemaphoreType.DMA((2,)), pltpu.SemaphoreType.DMA((2,))],
    )
    return pl.pallas_call(
        functools.partial(_moe_ffn_kernel, tm=tm, n_chunks=n_chunks, n_pairs=n_pairs),
        grid_spec=grid_spec,
        out_shape=jax.ShapeDtypeStruct((n_pairs + 2 * tm, d), F32),
        compiler_params=pltpu.CompilerParams(
            dimension_semantics=("arbitrary", "arbitrary"),
            vmem_limit_bytes=VMEM_LIMIT_BYTES),
        name="ffn_moe",
    )(tile_expert, tile_rows, src_token, src_token, dst_row, f, gate_row, w1, w3, w2)


def _ple_kernel(h1_ref, y0_ref, y1_ref, p_ref, pg_ref, wg_ref, wp_ref, o_ref):
    h2 = h1_ref[...] + y0_ref[...] + y1_ref[...]
    o_ref[...] = _ple(h2, p_ref[0, 0], pg_ref[...], wg_ref[...], wp_ref[...])


def _ple_combine(h1, y, p, layer, pg, wg, wp, *, tm):
    n, d = h1.shape
    row = lambda i: (i, 0)
    fixed = lambda i: (0, 0)
    return pl.pallas_call(
        _ple_kernel,
        grid=(n // tm,),
        in_specs=[
            pl.BlockSpec((tm, d), row),
            pl.BlockSpec((tm, d), row),
            pl.BlockSpec((tm, d), lambda i: (i + n // tm, 0)),
            pl.BlockSpec(*_layer_rows(p, layer, tm)),
            pl.BlockSpec((1, d), fixed),
            pl.BlockSpec(wg.shape, fixed),
            pl.BlockSpec(wp.shape, fixed),
        ],
        out_specs=pl.BlockSpec((tm, d), row),
        out_shape=jax.ShapeDtypeStruct((n, d), F32),
        compiler_params=pltpu.CompilerParams(
            dimension_semantics=("arbitrary",), vmem_limit_bytes=VMEM_LIMIT_BYTES),
        name="ple_combine",
    )(h1, y, y, p, pg, wg, wp)


def kernel(x, p, mix_norm_g, w_in, q_norm_g, k_norm_g, conv_w, conv_b, attn_out_g, conv_out_g,
           w_o, ffn_norm_g, dense_w1, dense_w3, dense_w2, router_w, moe_w1, moe_w3, moe_w2,
           ple_norm_g, ple_gate_w, ple_proj_w):
    bsz, seq_len, d = x.shape
    depth = w_in.shape[0]
    sb = attn_out_g.shape[1]
    n_heads = sb // HEAD_DIM
    n_experts = router_w.shape[2]
    n = bsz * seq_len
    tm = min(ROW_TILE, seq_len)
    t = min(ATTN_TILE, seq_len)
    tm_ffn = min(FFN_ROW_TILE, seq_len)

    head_id = jnp.arange(sb) // HEAD_DIM
    head_mean = jnp.where(head_id[:, None] == head_id[None, :], 1.0 / HEAD_DIM, 0.0).astype(BF16)
    ar = jnp.arange(t)
    tri = (ar[:, None] >= ar[None, :]).astype(BF16)
    q_scale = LOG2E / float(np.sqrt(HEAD_DIM))
    row2 = lambda a: a.reshape(1, -1)

    h = x.reshape(n, d)
    for i in range(depth):
        qg = row2(jnp.tile(q_norm_g[i], n_heads) * q_scale)
        kg = row2(jnp.tile(k_norm_g[i], n_heads))
        q, k, v, cu, b = _mix_in(h, row2(mix_norm_g[i]), w_in[i].astype(BF16), qg, kg,
                                 head_mean, tm=tm, seq_len=seq_len)
        attn = _attention(q, k, v, tri, t=t, pairs=ATTN_PAIRS)
        j = i // 2
        routed = i % 2 == 1
        rw = None
        if routed:
            e_pad = -(-n_experts // SUBLANES) * SUBLANES
            rw = jnp.pad(router_w[j].T, ((0, e_pad - n_experts), (0, 0)))
            rw_hi = rw.astype(BF16)
            rw = jnp.stack([rw_hi, (rw - rw_hi.astype(F32)).astype(BF16)])
        outs = _mix_out(attn, cu, b, h, row2(attn_out_g[i]), row2(conv_out_g[i]), conv_w[i],
                        row2(conv_b[i]), w_o[i].astype(BF16), row2(ffn_norm_g[i]), rw,
                        tm=tm, seq_len=seq_len, n_experts=n_experts if routed else 0)
        ple_w = (row2(ple_norm_g[i]), ple_gate_w[i].astype(BF16), ple_proj_w[i].astype(BF16))
        if routed:
            h1, f, route = outs
            plan = _route_plan(route, n_experts, MOE_ROW_TILE)
            y = _moe_ffn(f, plan, moe_w1[j].astype(BF16), moe_w3[j].astype(BF16),
                         moe_w2[j].astype(BF16), tm=MOE_ROW_TILE, tf=MOE_COL_TILE)
            h = _ple_combine(h1, y, p, i, *ple_w, tm=tm)
        else:
            h1, f = outs
            h = _ffn(f, dense_w1[j:j + 1].astype(BF16), dense_w3[j:j + 1].astype(BF16),
                     dense_w2[j:j + 1].astype(BF16), h1, p, i, *ple_w, tm=tm_ffn,
                     tf=FFN_COL_TILE)
    return h.reshape(bsz, seq_len, d)
```

```python
import functools

import numpy as np
import jax
import jax.numpy as jnp
from jax import lax
from jax.experimental import pallas as pl
from jax.experimental.pallas import tpu as pltpu

F32 = jnp.float32
BF16 = jnp.bfloat16

EPS = 1e-6
HEAD_DIM = 64
CONV_K = 3
TOP_K = 2
LANES = 128
SUBLANES = 8
HALO_ROWS = 2 * SUBLANES
LOG2E = float(np.log2(np.e))

VMEM_LIMIT_BYTES = 56 * 1024 * 1024

ROW_TILE = 1024
ATTN_TILE = 256
ATTN_PAIRS = 4
SP_CLAMP = 64.0
DONE_LOG2 = 160.0
NO_BLOCK_DECAY = 1e30
FFN_ROW_TILE = 512
FFN_COL_TILE = 3584
MOE_ROW_TILE = 512
MOE_COL_TILE = 3584


def _rms(x, g):
    return x * lax.rsqrt(jnp.mean(x * x, axis=-1, keepdims=True) + EPS) * g


def _dot(a, b):
    return jnp.dot(a, b, preferred_element_type=F32)


def _mix_in_kernel(h_ref, g_ref, w_ref, qg_ref, kg_ref, hm_ref,
                   q_ref, k_ref, v_ref, cu_ref, b_ref, *, sb):
    a = _rms(h_ref[...], g_ref[...]).astype(BF16)
    hm = hm_ref[...]

    def qk_norm(t, g):
        ms = _dot((t * t).astype(BF16), hm)
        return (t * lax.rsqrt(ms + EPS) * g).astype(BF16)

    q_ref[0] = qk_norm(_dot(a, w_ref[:, 0:sb]), qg_ref[...])
    k_ref[0] = qk_norm(_dot(a, w_ref[:, sb:2 * sb]), kg_ref[...])
    v_ref[0] = _dot(a, w_ref[:, 2 * sb:3 * sb]).astype(BF16)
    u = _dot(a, w_ref[:, 3 * sb:4 * sb])
    c = _dot(a, w_ref[:, 4 * sb:5 * sb])
    cu_ref[...] = (c * u).astype(BF16)
    b_ref[...] = _dot(a, w_ref[:, 5 * sb:6 * sb]).astype(BF16)


def _mix_in(h, g, w_in, qg, kg, head_mean, *, tm, seq_len):
    n, d = h.shape
    sb = head_mean.shape[0]
    tiles_per_seq = seq_len // tm
    row = lambda i: (i, 0)
    fixed = lambda i: (0, 0)
    seq_row = lambda i: (i // tiles_per_seq, i % tiles_per_seq, 0)
    out_bf = jax.ShapeDtypeStruct((n // seq_len, seq_len, sb), BF16)
    out_rows = jax.ShapeDtypeStruct((n, sb), BF16)
    return pl.pallas_call(
        functools.partial(_mix_in_kernel, sb=sb),
        grid=(n // tm,),
        in_specs=[
            pl.BlockSpec((tm, d), row),
            pl.BlockSpec((1, d), fixed),
            pl.BlockSpec(w_in.shape, fixed),
            pl.BlockSpec((1, sb), fixed),
            pl.BlockSpec((1, sb), fixed),
            pl.BlockSpec((sb, sb), fixed),
        ],
        out_specs=[pl.BlockSpec((1, tm, sb), seq_row)] * 3 + [pl.BlockSpec((tm, sb), row)] * 2,
        out_shape=[out_bf, out_bf, out_bf, out_rows, out_rows],
        compiler_params=pltpu.CompilerParams(
            dimension_semantics=("arbitrary",), vmem_limit_bytes=VMEM_LIMIT_BYTES),
        name="mix_in",
    )(h, g, w_in, qg, kg, head_mean)


def _attn_kernel(q_ref, k_ref, v_ref, u_ref, o_ref, acc_ref, c_ref, *, t, pairs):
    i = pl.program_id(2)
    heads = 2 * pairs
    q = q_ref[0]
    lane = lax.broadcasted_iota(jnp.int32, (t, LANES), 1)
    lo_half = lane < HEAD_DIM
    q_heads = []
    for p in range(pairs):
        qp = q[:, p * LANES:(p + 1) * LANES]
        q_heads += [jnp.where(lo_half, qp, jnp.zeros_like(qp)),
                    jnp.where(lo_half, jnp.zeros_like(qp), qp)]
    tri = u_ref[...]
    row = lax.broadcasted_iota(jnp.int32, (t, t), 0)
    col = lax.broadcasted_iota(jnp.int32, (t, t), 1)
    causal = col < row

    acc_ref[...] = jnp.zeros_like(acc_ref)
    c_ref[...] = jnp.zeros_like(c_ref)

    def block(j, diagonal, extra_decay=0.0):
        start = pl.multiple_of(j * t, t)
        kj = k_ref[0, pl.ds(start, t), :]
        vj = v_ref[0, pl.ds(start, t), :]
        for hd in range(heads):
            p = hd // 2
            z = lax.dot_general(q_heads[hd], kj[:, p * LANES:(p + 1) * LANES],
                                (((1,), (1,)), ((), ())), preferred_element_type=F32)
            sp = jnp.maximum(z, jnp.log2(1.0 + jnp.exp2(jnp.minimum(z, SP_CLAMP))))
            if diagonal:
                sp = jnp.where(causal, sp, 0.0)
            incl = _dot(sp.astype(BF16), tri)
            w = jnp.exp2(z - incl - (c_ref[hd] + extra_decay))
            if diagonal:
                w = jnp.where(causal, w, 0.0)
            acc_ref[hd] += _dot(w.astype(BF16), vj[:, p * LANES:(p + 1) * LANES])
            c_ref[hd] += incl[:, 0:1]

    def decay_floor():
        m = jnp.min(c_ref[0])
        for hd in range(1, heads):
            m = jnp.minimum(m, jnp.min(c_ref[hd]))
        return m

    block(i, True)
    block(jnp.maximum(i - 1, 0), False, extra_decay=jnp.where(i > 0, 0.0, NO_BLOCK_DECAY))

    def cond(carry):
        j, floor = carry
        return (j >= 0) & (floor < DONE_LOG2)

    def body(carry):
        j, _ = carry
        block(j, False)
        return j - 1, decay_floor()

    lax.while_loop(cond, body, (i - 2, decay_floor()))
    for p in range(pairs):
        o_ref[0, :, p * LANES:(p + 1) * LANES] = jnp.where(
            lo_half, acc_ref[2 * p], acc_ref[2 * p + 1]).astype(o_ref.dtype)


def _attention(q, k, v, tri, *, t, pairs):
    bsz, s, sb = q.shape
    width = LANES * pairs
    blk = lambda b, g, i: (b, i, g)
    seq = lambda b, g, i: (b, 0, g)
    return pl.pallas_call(
        functools.partial(_attn_kernel, t=t, pairs=pairs),
        grid=(bsz, sb // width, s // t),
        in_specs=[
            pl.BlockSpec((1, t, width), blk),
            pl.BlockSpec((1, s, width), seq),
            pl.BlockSpec((1, s, width), seq),
            pl.BlockSpec((t, t), lambda b, g, i: (0, 0)),
        ],
        out_specs=pl.BlockSpec((1, t, width), blk),
        out_shape=jax.ShapeDtypeStruct((bsz, s, sb), BF16),
        scratch_shapes=[pltpu.VMEM((2 * pairs, t, LANES), F32),
                        pltpu.VMEM((2 * pairs, t, 1), F32)],
        compiler_params=pltpu.CompilerParams(
            dimension_semantics=("arbitrary", "arbitrary", "arbitrary"),
            vmem_limit_bytes=VMEM_LIMIT_BYTES),
        name="attention",
    )(q, k, v, tri)


def _mix_out_kernel(*refs, tiles_per_seq, sb, n_experts):
    (attn_ref, cu_ref, halo_ref, b_ref, h_ref, ag_ref, cg_ref, cw_ref, cb_ref,
     wo_ref, fg_ref) = refs[:11]
    if n_experts:
        rw_ref, h1_ref, f_ref, route_ref = refs[11:]
    else:
        h1_ref, f_ref = refs[11:]

    attn_n = _rms(attn_ref[0].astype(F32), ag_ref[...])

    cu = cu_ref[...].astype(F32)
    first = pl.program_id(0) % tiles_per_seq == 0
    halo = jnp.where(first, 0.0, halo_ref[...].astype(F32))
    rowi = lax.broadcasted_iota(jnp.int32, cu.shape, 0)
    prev1 = jnp.where(rowi == 0, halo[HALO_ROWS - 1:HALO_ROWS], pltpu.roll(cu, 1, axis=0))
    prev2 = jnp.where(rowi == 0, halo[HALO_ROWS - 2:HALO_ROWS - 1],
                      jnp.where(rowi == 1, halo[HALO_ROWS - 1:HALO_ROWS],
                                pltpu.roll(cu, 2, axis=0)))
    conv = cw_ref[0:1] * prev2 + cw_ref[1:2] * prev1 + cw_ref[2:3] * cu + cb_ref[...]
    conv_n = _rms(b_ref[...].astype(F32) * conv, cg_ref[...])

    h1 = (h_ref[...] + _dot(attn_n.astype(BF16), wo_ref[0:sb])
          + _dot(conv_n.astype(BF16), wo_ref[sb:]))
    h1_ref[...] = h1
    f = _rms(h1, fg_ref[...])
    f_ref[...] = f.astype(f_ref.dtype)

    if n_experts:
        f_hi = f.astype(BF16)
        f_lo = (f - f_hi.astype(F32)).astype(BF16)
        nt = (((1,), (1,)), ((), ()))
        logits = (lax.dot_general(rw_ref[0], f_hi, nt, preferred_element_type=F32)
                  + (lax.dot_general(rw_ref[0], f_lo, nt, preferred_element_type=F32)
                     + lax.dot_general(rw_ref[1], f_hi, nt, preferred_element_type=F32)))
        e_pad = logits.shape[0]
        eid = lax.broadcasted_iota(jnp.int32, logits.shape, 0).astype(F32)
        neg = jnp.float32(-jnp.inf)
        logits = jnp.where(eid < n_experts, logits, neg)
        m1 = jnp.max(logits, axis=0, keepdims=True)
        i1 = jnp.min(jnp.where(logits == m1, eid, float(e_pad)), axis=0, keepdims=True)
        rest = jnp.where(eid == i1, neg, logits)
        m2 = jnp.max(rest, axis=0, keepdims=True)
        i2 = jnp.min(jnp.where(rest == m2, eid, float(e_pad)), axis=0, keepdims=True)
        e2 = jnp.exp(m2 - m1)
        g1 = 1.0 / (1.0 + e2)
        g2 = e2 * g1
        slot = lax.broadcasted_iota(jnp.int32, route_ref.shape, 0)
        route_ref[...] = jnp.where(
            slot == 0, i1, jnp.where(slot == 1, i2, jnp.where(
                slot == 2, g1, jnp.where(slot == 3, g2, 0.0))))


def _mix_out(attn, cu, b, h, ag, cg, cw, cb, w_o, fg, router_w, *, tm, seq_len, n_experts):
    n, d = h.shape
    sb = attn.shape[2]
    tiles_per_seq = seq_len // tm
    row = lambda i: (i, 0)
    fixed = lambda i: (0, 0)
    halo = lambda i: (jnp.maximum(i * (tm // HALO_ROWS) - 1, 0), 0)
    in_specs = [
        pl.BlockSpec((1, tm, sb), lambda i: (i // tiles_per_seq, i % tiles_per_seq, 0)),
        pl.BlockSpec((tm, sb), row),
        pl.BlockSpec((HALO_ROWS, sb), halo),
        pl.BlockSpec((tm, sb), row),
        pl.BlockSpec((tm, d), row),
        pl.BlockSpec((1, sb), fixed),
        pl.BlockSpec((1, sb), fixed),
        pl.BlockSpec((CONV_K, sb), fixed),
        pl.BlockSpec((1, sb), fixed),
        pl.BlockSpec(w_o.shape, fixed),
        pl.BlockSpec((1, d), fixed),
    ]
    args = [attn, cu, cu, b, h, ag, cg, cw, cb, w_o, fg]
    out_specs = [pl.BlockSpec((tm, d), row), pl.BlockSpec((tm, d), row)]
    out_shape = [jax.ShapeDtypeStruct((n, d), F32),
                 jax.ShapeDtypeStruct((n, d), F32 if n_experts else BF16)]
    if n_experts:
        in_specs.append(pl.BlockSpec(router_w.shape, lambda i: (0, 0, 0)))
        args.append(router_w)
        out_specs.append(pl.BlockSpec((SUBLANES, tm), lambda i: (0, i)))
        out_shape.append(jax.ShapeDtypeStruct((SUBLANES, n), F32))
    return pl.pallas_call(
        functools.partial(_mix_out_kernel, tiles_per_seq=seq_len // tm, sb=sb,
                          n_experts=n_experts),
        grid=(n // tm,),
        in_specs=in_specs,
        out_specs=out_specs,
        out_shape=out_shape,
        compiler_params=pltpu.CompilerParams(
            dimension_semantics=("arbitrary",), vmem_limit_bytes=VMEM_LIMIT_BYTES),
        name="mix_out_moe" if n_experts else "mix_out",
    )(*args)


def _swiglu_step(x, w1_ref, w3_ref, w2_ref):
    a = _dot(x, w1_ref[0])
    hidden = a * jax.nn.sigmoid(a) * _dot(x, w3_ref[0])
    return _dot(hidden.astype(BF16), w2_ref[0])


def _ple(h2, p, pg, wg, wp):
    gate = jax.nn.sigmoid(_dot(_rms(h2, pg).astype(BF16), wg))
    return h2 + gate * _dot(p.astype(BF16), wp)


def _ffn_kernel(f_ref, w1_ref, w3_ref, w2_ref, h1_ref, p_ref, pg_ref, wg_ref, wp_ref,
                o_ref, acc_ref):
    c = pl.program_id(1)

    @pl.when(c == 0)
    def _():
        acc_ref[...] = jnp.zeros_like(acc_ref)

    acc_ref[...] += _swiglu_step(f_ref[...], w1_ref, w3_ref, w2_ref)

    @pl.when(c == pl.num_programs(1) - 1)
    def _():
        o_ref[...] = _ple(h1_ref[...] + acc_ref[...], p_ref[0, 0], pg_ref[...], wg_ref[...],
                          wp_ref[...])


def _layer_rows(p, layer, tm):
    tiles_per_seq = p.shape[2] // tm
    return ((1, 1, tm, p.shape[3]),
            lambda i, *_: (layer, i // tiles_per_seq, i % tiles_per_seq, 0))


def _ffn(f, w1, w3, w2, h1, p, layer, pg, wg, wp, *, tm, tf):
    n, d = h1.shape
    d_ff = w1.shape[2]
    row = lambda i, c: (i, 0)
    fixed = lambda i, c: (0, 0)
    once = dict(pipeline_mode=pl.Buffered(1))
    w_once = once if tf == d_ff else {}
    in_specs = [
        pl.BlockSpec((tm, d), row),
        pl.BlockSpec((1, d, tf), lambda i, c: (0, 0, c), **w_once),
        pl.BlockSpec((1, d, tf), lambda i, c: (0, 0, c), **w_once),
        pl.BlockSpec((1, tf, d), lambda i, c: (0, c, 0), **w_once),
        pl.BlockSpec((tm, d), row),
        pl.BlockSpec(*_layer_rows(p, layer, tm)),
        pl.BlockSpec((1, d), fixed, **once),
        pl.BlockSpec(wg.shape, fixed, **once),
        pl.BlockSpec(wp.shape, fixed, **once),
    ]
    return pl.pallas_call(
        _ffn_kernel,
        grid=(n // tm, d_ff // tf),
        in_specs=in_specs,
        out_specs=pl.BlockSpec((tm, d), row),
        out_shape=jax.ShapeDtypeStruct((n, d), F32),
        scratch_shapes=[pltpu.VMEM((tm, d), F32)],
        compiler_params=pltpu.CompilerParams(
            dimension_semantics=("arbitrary", "arbitrary"),
            vmem_limit_bytes=VMEM_LIMIT_BYTES),
        name="ffn_dense",
    )(f, w1, w3, w2, h1, p, pg, wg, wp)


def _route_plan(route, n_experts, tm):
    n = route.shape[1]
    pairs = TOP_K * n
    n_rows = pairs + n_experts * tm
    eid = route[0:TOP_K].astype(jnp.int32).reshape(pairs)
    gates = route[TOP_K:2 * TOP_K].reshape(pairs)
    pair_bits = int(np.ceil(np.log2(pairs)))
    keys = jnp.sort(eid * (1 << pair_bits) + jnp.arange(pairs, dtype=jnp.int32))
    order = keys & ((1 << pair_bits) - 1)
    bounds = jnp.arange(n_experts + 1, dtype=jnp.int32) << pair_bits
    bounds = bounds.at[n_experts].set(jnp.iinfo(jnp.int32).max)
    first = jnp.sum(keys[None, :] < bounds[:, None], axis=1).astype(jnp.int32)
    counts = first[1:] - first[:-1]
    padded = (counts + tm - 1) // tm * tm
    ends = jnp.cumsum(padded)
    n_tiles = n_rows // tm
    tile_start = jnp.arange(n_tiles, dtype=jnp.int32) * tm
    tile_expert = jnp.sum(tile_start[:, None] >= ends[None, :], axis=1)
    tile_expert = jnp.minimum(tile_expert, n_experts - 1).astype(jnp.int32)
    in_group = tile_start - (ends - padded)[tile_expert]
    tile_rows = jnp.clip(counts[tile_expert] - in_group, 0, tm)
    lane = jnp.arange(tm, dtype=jnp.int32)[None, :]
    real = lane < tile_rows[:, None]
    safe = order[jnp.clip((first[tile_expert] + in_group)[:, None] + lane, 0, pairs - 1)]
    src_token = jnp.where(real, safe % n, 0)
    gate_row = jnp.where(real, gates[safe], 0.0)
    spare = pairs + (jnp.arange(n_tiles, dtype=jnp.int32) % 2 * tm)[:, None] + lane
    dst_row = jnp.where(real, safe, spare)
    src_token = jnp.concatenate([src_token, jnp.zeros((1, tm), jnp.int32)])
    dst_row = jnp.concatenate([pairs + tm + lane, dst_row])
    return (src_token.reshape(n_tiles + 1, 1, tm), dst_row.reshape(n_tiles + 1, 1, tm),
            jnp.broadcast_to(gate_row[:, None, :], (n_tiles, SUBLANES, tm)), tile_expert,
            tile_rows)


def _for_each_row(count, fn, unroll=8):
    assert count % unroll == 0

    def chunk(i, carry):
        for u in range(unroll):
            fn(i * unroll + u)
        return carry

    lax.fori_loop(0, count // unroll, chunk, 0)


def _moe_ffn_kernel(expert_ref, count_ref, src0_ref, src_ref, dst_ref, f_hbm, gate_ref,
                    w1_ref, w3_ref, w2_ref, y_hbm, rows_ref, x_ref, acc_ref, stage_ref,
                    gather_sem, scatter_sem, *, tm, n_chunks, n_pairs):
    j = pl.program_id(0)
    c = pl.program_id(1)
    last_tile = pl.num_programs(0) - 1
    last_chunk = n_chunks - 1
    per_step = tm // n_chunks
    active = count_ref[j] > 0
    cur = j % 2
    other = 1 - cur

    def tile_gather(slot):
        return pltpu.make_async_copy(f_hbm.at[pl.ds(0, tm), :], rows_ref.at[slot],
                                     gather_sem.at[slot])

    def tile_scatter(slot):
        return pltpu.make_async_copy(stage_ref.at[slot], y_hbm.at[pl.ds(0, tm), :],
                                     scatter_sem.at[slot])

    @pl.when((j == 0) & (c == 0))
    def _():
        stage_ref[...] = jnp.zeros_like(stage_ref)
        for slot in range(2):
            spare = pltpu.make_async_copy(stage_ref.at[slot],
                                          y_hbm.at[pl.ds(n_pairs + slot * tm, tm), :],
                                          scatter_sem.at[slot])
            spare.start()
            spare.wait()

        def gather_first(r):
            pltpu.make_async_copy(f_hbm.at[pl.ds(src0_ref[0, 0, r], 1), :],
                                  rows_ref.at[0, pl.ds(r, 1), :], gather_sem.at[0]).start()
        _for_each_row(tm, gather_first)

    @pl.when(c == 0)
    def _():
        tile_gather(cur).wait()

        @pl.when(j >= 1)
        def _():
            tile_scatter(cur).wait()

        x_ref[...] = rows_ref[cur].astype(BF16)
        if n_chunks > 1:
            acc_ref[...] = jnp.zeros_like(acc_ref)

    def start_row_copies():
        base = pl.multiple_of(c * per_step, per_step)
        for r in range(per_step):
            pltpu.make_async_copy(f_hbm.at[pl.ds(src_ref[0, 0, base + r], 1), :],
                                  rows_ref.at[other, pl.ds(base + r, 1), :],
                                  gather_sem.at[other]).start()
            pltpu.make_async_copy(stage_ref.at[other, pl.ds(base + r, 1), :],
                                  y_hbm.at[pl.ds(dst_ref[0, 0, base + r], 1), :],
                                  scatter_sem.at[other]).start()

    @pl.when(active)
    def _():
        start_row_copies()
        if n_chunks == 1:
            gate = jnp.transpose(gate_ref[0])[:, 0:1]
            stage_ref[cur] = _swiglu_step(x_ref[...], w1_ref, w3_ref, w2_ref) * gate
        else:
            acc_ref[...] += _swiglu_step(x_ref[...], w1_ref, w3_ref, w2_ref)

    @pl.when(jnp.logical_not(active))
    def _():
        start_row_copies()

    @pl.when(active & (c == last_chunk) & (n_chunks > 1))
    def _():
        gate = jnp.transpose(gate_ref[0])[:, 0:1]
        stage_ref[cur] = acc_ref[...] * gate

    @pl.when((j == last_tile) & (c == last_chunk))
    def _():
        tile_gather(other).wait()
        tile_scatter(other).wait()


def _moe_ffn(f, plan, w1, w3, w2, *, tm, tf):
    src_token, dst_row, gate_row, tile_expert, tile_rows = plan
    n, d = f.shape
    n_exp, _, d_ff = w1.shape
    n_tiles = tile_rows.shape[0]
    n_chunks = d_ff // tf
    n_pairs = TOP_K * n
    assert n_pairs % tm == 0 and tm % n_chunks == 0 and n_exp < tm
    chunk = lambda j, c, count: jnp.where(count[j] > 0, c, n_chunks - 1)
    w_once = dict(pipeline_mode=pl.Buffered(1)) if tf == d_ff else {}
    grid_spec = pltpu.PrefetchScalarGridSpec(
        num_scalar_prefetch=2,
        grid=(n_tiles, n_chunks),
        in_specs=[
            pl.BlockSpec((1, 1, tm), lambda j, c, ex, count: (0, 0, 0), memory_space=pltpu.SMEM),
            pl.BlockSpec((1, 1, tm), lambda j, c, ex, count: (j + 1, 0, 0),
                         memory_space=pltpu.SMEM),
            pl.BlockSpec((1, 1, tm), lambda j, c, ex, count: (j, 0, 0), memory_space=pltpu.SMEM),
            pl.BlockSpec(memory_space=pl.ANY),
            pl.BlockSpec((1, SUBLANES, tm), lambda j, c, ex, count: (j, 0, 0)),
            pl.BlockSpec((1, d, tf), lambda j, c, ex, count: (ex[j], 0, chunk(j, c, count)),
                         **w_once),
            pl.BlockSpec((1, d, tf), lambda j, c, ex, count: (ex[j], 0, chunk(j, c, count)),
                         **w_once),
            pl.BlockSpec((1, tf, d), lambda j, c, ex, count: (ex[j], chunk(j, c, count), 0),
                         **w_once),
        ],
        out_specs=pl.BlockSpec(memory_space=pl.ANY),
        scratch_shapes=[pltpu.VMEM((2, tm, d), F32), pltpu.VMEM((tm, d), BF16),
                        pltpu.VMEM((tm, d), F32), pltpu.VMEM((2, tm, d), F32),
                        pltpu.SemaphoreType.DMA((2,)), pltpu.SemaphoreType.DMA((2,))],
    )
    return pl.pallas_call(
        functools.partial(_moe_ffn_kernel, tm=tm, n_chunks=n_chunks, n_pairs=n_pairs),
        grid_spec=grid_spec,
        out_shape=jax.ShapeDtypeStruct((n_pairs + 2 * tm, d), F32),
        compiler_params=pltpu.CompilerParams(
            dimension_semantics=("arbitrary", "arbitrary"),
            vmem_limit_bytes=VMEM_LIMIT_BYTES),
        name="ffn_moe",
    )(tile_expert, tile_rows, src_token, src_token, dst_row, f, gate_row, w1, w3, w2)


def _ple_kernel(h1_ref, y0_ref, y1_ref, p_ref, pg_ref, wg_ref, wp_ref, o_ref):
    h2 = h1_ref[...] + y0_ref[...] + y1_ref[...]
    o_ref[...] = _ple(h2, p_ref[0, 0], pg_ref[...], wg_ref[...], wp_ref[...])


def _ple_combine(h1, y, p, layer, pg, wg, wp, *, tm):
    n, d = h1.shape
    row = lambda i: (i, 0)
    fixed = lambda i: (0, 0)
    return pl.pallas_call(
        _ple_kernel,
        grid=(n // tm,),
        in_specs=[
            pl.BlockSpec((tm, d), row),
            pl.BlockSpec((tm, d), row),
            pl.BlockSpec((tm, d), lambda i: (i + n // tm, 0)),
            pl.BlockSpec(*_layer_rows(p, layer, tm)),
            pl.BlockSpec((1, d), fixed),
            pl.BlockSpec(wg.shape, fixed),
            pl.BlockSpec(wp.shape, fixed),
        ],
        out_specs=pl.BlockSpec((tm, d), row),
        out_shape=jax.ShapeDtypeStruct((n, d), F32),
        compiler_params=pltpu.CompilerParams(
            dimension_semantics=("arbitrary",), vmem_limit_bytes=VMEM_LIMIT_BYTES),
        name="ple_combine",
    )(h1, y, y, p, pg, wg, wp)


def kernel(x, p, mix_norm_g, w_in, q_norm_g, k_norm_g, conv_w, conv_b, attn_out_g, conv_out_g,
           w_o, ffn_norm_g, dense_w1, dense_w3, dense_w2, router_w, moe_w1, moe_w3, moe_w2,
           ple_norm_g, ple_gate_w, ple_proj_w):
    bsz, seq_len, d = x.shape
    depth = w_in.shape[0]
    sb = attn_out_g.shape[1]
    n_heads = sb // HEAD_DIM
    n_experts = router_w.shape[2]
    n = bsz * seq_len
    tm = min(ROW_TILE, seq_len)
    t = min(ATTN_TILE, seq_len)
    tm_ffn = min(FFN_ROW_TILE, seq_len)

    head_id = jnp.arange(sb) // HEAD_DIM
    head_mean = jnp.where(head_id[:, None] == head_id[None, :], 1.0 / HEAD_DIM, 0.0).astype(BF16)
    ar = jnp.arange(t)
    tri = (ar[:, None] >= ar[None, :]).astype(BF16)
    q_scale = LOG2E / float(np.sqrt(HEAD_DIM))
    row2 = lambda a: a.reshape(1, -1)

    h = x.reshape(n, d)
    for i in range(depth):
        qg = row2(jnp.tile(q_norm_g[i], n_heads) * q_scale)
        kg = row2(jnp.tile(k_norm_g[i], n_heads))
        q, k, v, cu, b = _mix_in(h, row2(mix_norm_g[i]), w_in[i].astype(BF16), qg, kg,
                                 head_mean, tm=tm, seq_len=seq_len)
        attn = _attention(q, k, v, tri, t=t, pairs=ATTN_PAIRS)
        j = i // 2
        routed = i % 2 == 1
        rw = None
        if routed:
            e_pad = -(-n_experts // SUBLANES) * SUBLANES
            rw = jnp.pad(router_w[j].T, ((0, e_pad - n_experts), (0, 0)))
            rw_hi = rw.astype(BF16)
            rw = jnp.stack([rw_hi, (rw - rw_hi.astype(F32)).astype(BF16)])
        outs = _mix_out(attn, cu, b, h, row2(attn_out_g[i]), row2(conv_out_g[i]), conv_w[i],
                        row2(conv_b[i]), w_o[i].astype(BF16), row2(ffn_norm_g[i]), rw,
                        tm=tm, seq_len=seq_len, n_experts=n_experts if routed else 0)
        ple_w = (row2(ple_norm_g[i]), ple_gate_w[i].astype(BF16), ple_proj_w[i].astype(BF16))
        if routed:
            h1, f, route = outs
            plan = _route_plan(route, n_experts, MOE_ROW_TILE)
            y = _moe_ffn(f, plan, moe_w1[j].astype(BF16), moe_w3[j].astype(BF16),
                         moe_w2[j].astype(BF16), tm=MOE_ROW_TILE, tf=MOE_COL_TILE)
            h = _ple_combine(h1, y, p, i, *ple_w, tm=tm)
        else:
            h1, f = outs
            h = _ffn(f, dense_w1[j:j + 1].astype(BF16), dense_w3[j:j + 1].astype(BF16),
                     dense_w2[j:j + 1].astype(BF16), h1, p, i, *ple_w, tm=tm_ffn,
                     tf=FFN_COL_TILE)
    return h.reshape(bsz, seq_len, d)
```
